```python
import math
import jax, jax.numpy as jnp
from jax import lax
import numpy as np

D_MODEL = 1024
BATCH = 8
SEQ = 2048
DEPTH = 2
DEC_BATCH = 1
DEC_SEQ = 16384
PAST_LEN = 128

N_META = 16
N_HEADS = 8
N_KV_HEADS = 2
GROUP = N_HEADS // N_KV_HEADS
HEAD_DIM = 64
ATTN_WIDTH = N_HEADS * HEAD_DIM
KV_WIDTH = N_KV_HEADS * HEAD_DIM
WINDOW = 128
BLOCK = 128
N_BUCKETS = 32
MAX_DISTANCE = 128
HYENA_WIDTH = D_MODEL // 2
HYENA_EMB = 33
HYENA_FILTER_HIDDEN = 64
HYENA_DECAY_TARGET = 1e-2
HYENA_FAST_DECAY = 0.3
HYENA_SLOW_DECAY = 1.5
POOL_WIDTH = D_MODEL // 2
POOL_WINDOWS = (2, 4, 8, 16)
POOL_GROUPS = len(POOL_WINDOWS)
POOL_GROUP = POOL_WIDTH // POOL_GROUPS
N_BRANCH = 3
D_FF = 2816
EPS = 1e-6
NEG = -1e30
IN_WIDTH = ATTN_WIDTH + 2 * KV_WIDTH + 3 * HYENA_WIDTH + POOL_WIDTH + N_BRANCH * D_MODEL

kernel_name = "hybrid_hyena_swa_pool_encoder"


def rmsnorm(x, g):
    xf = x.astype(jnp.float32)
    y = xf * lax.rsqrt(jnp.mean(xf * xf, axis=-1, keepdims=True) + EPS)
    return (y * g.astype(jnp.float32)).astype(x.dtype)


def dwconv3(x, w, b):
    xp = jnp.pad(x, ((0, 0), (1, 1), (0, 0)))
    return xp[:, :-2] * w[0] + xp[:, 1:-1] * w[1] + xp[:, 2:] * w[2] + b


def t5_buckets(rel):
    half = N_BUCKETS // 2
    ret = (rel > 0).astype(np.int32) * half
    n = np.abs(rel)
    max_exact = half // 2
    large = max_exact + (np.log(np.maximum(n, 1) / max_exact) / np.log(MAX_DISTANCE / max_exact)
                         * (half - max_exact)).astype(np.int32)
    large = np.minimum(large, half - 1)
    return (ret + np.where(n < max_exact, n, large)).astype(np.int32)


def window_attention(q, k, v, q_gain, k_gain, sink, rel_bias):
    B, L = q.shape[0], q.shape[1]
    front = (-L) % BLOCK
    Lp = L + front
    nb = Lp // BLOCK
    q = rmsnorm(q, q_gain)
    k = rmsnorm(k, k_gain)
    padf = lambda a: jnp.pad(a, ((0, 0), (front, 0), (0, 0), (0, 0)))
    qb = padf(q).reshape(B, nb, BLOCK, N_KV_HEADS, GROUP, HEAD_DIM)

    def windows(a):
        a = padf(a).reshape(B, nb, BLOCK, N_KV_HEADS, HEAD_DIM)
        a = jnp.pad(a, ((0, 0), (1, 1), (0, 0), (0, 0), (0, 0)))
        return jnp.concatenate([a[:, :-2], a[:, 1:-1], a[:, 2:]], axis=2)

    kw, vw = windows(k), windows(v)
    rel = np.arange(3 * BLOCK)[None, :] - BLOCK - np.arange(BLOCK)[:, None]
    band = np.abs(rel) <= WINDOW
    bias = rel_bias[t5_buckets(rel)].astype(jnp.float32)
    bias = jnp.transpose(bias, (2, 0, 1)).reshape(N_KV_HEADS, GROUP, BLOCK, 3 * BLOCK)
    kpos = (np.arange(nb)[:, None] - 1) * BLOCK + np.arange(3 * BLOCK)[None, :]
    kvalid = (kpos >= front) & (kpos < Lp)
    mask = band[None] & kvalid[:, None, :]

    s = jnp.einsum('bnqhgd,bnkhd->bnhgqk', qb, kw).astype(jnp.float32) * (HEAD_DIM ** -0.5)
    s = jnp.where(mask[None, :, None, None], s + bias, NEG)
    sink_logit = jnp.broadcast_to(sink.astype(jnp.float32).reshape(N_KV_HEADS, GROUP, 1, 1),
                                  s.shape[:-1] + (1,))
    p = jax.nn.softmax(jnp.concatenate([s, sink_logit], axis=-1), axis=-1)[..., :-1]
    o = jnp.einsum('bnhgqk,bnkhd->bnqhgd', p.astype(vw.dtype), vw)
    return o.reshape(B, Lp, ATTN_WIDTH)[:, front:]


def hyena_filter(L, w1, b1, w2, b2, w3, freq):
    t = jnp.linspace(0.0, 1.0, L, dtype=jnp.float32)[:, None]
    bands = (HYENA_EMB - 1) // 2
    f = jnp.linspace(1e-4, bands - 1, bands, dtype=jnp.float32)[None, :]
    wpos = (2.0 * math.pi / L) * jnp.arange(L, dtype=jnp.float32)[:, None]
    z = jnp.concatenate([t, jnp.cos(f * wpos), -jnp.sin(f * wpos)], axis=-1)
    h = jnp.sin(freq * (z @ w1 + b1))
    h = jnp.sin(freq * (h @ w2 + b2))
    kf = (h @ w3).astype(jnp.float32)
    deltas = jnp.abs(jnp.linspace(math.log(HYENA_DECAY_TARGET) / HYENA_FAST_DECAY,
                                  math.log(HYENA_DECAY_TARGET) / HYENA_SLOW_DECAY,
                                  HYENA_WIDTH, dtype=jnp.float32))
    decay = jnp.exp(-t * deltas[None, :])
    k_fwd = kf[:, :HYENA_WIDTH] * decay
    k_bwd = kf[:, HYENA_WIDTH:] * decay
    K = jnp.concatenate([k_fwd, jnp.zeros((1, HYENA_WIDTH), jnp.float32), k_bwd[:0:-1]], axis=0)
    return K / jnp.sum(jnp.abs(K), axis=0, keepdims=True)


def hyena_mixer(u, conv_w, conv_b, w1, b1, w2, b2, w3, freq, skip):
    L = u.shape[1]
    u = dwconv3(u, conv_w, conv_b)
    x0 = u[..., :HYENA_WIDTH]
    x1 = u[..., HYENA_WIDTH:2 * HYENA_WIDTH]
    v = u[..., 2 * HYENA_WIDTH:]
    z = (v * x1).astype(jnp.float32)
    K = hyena_filter(L, w1, b1, w2, b2, w3, freq)
    n = 2 * L
    y = jnp.fft.irfft(jnp.fft.rfft(z, n=n, axis=1) * jnp.fft.rfft(K, n=n, axis=0)[None],
                      n=n, axis=1)[:, :L]
    y = y + z * skip.astype(jnp.float32)
    return (y * x0.astype(jnp.float32)).astype(u.dtype)


def pool_mixer(u, w, scale):
    B, L, _ = u.shape
    uf = u.astype(jnp.float32).reshape(B, L, POOL_GROUPS, POOL_GROUP)
    c = jnp.pad(jnp.cumsum(uf, axis=1), ((0, 0), (1, 0), (0, 0), (0, 0)))
    t = np.arange(L)
    outs = []
    for g, win in enumerate(POOL_WINDOWS):
        lo = np.clip(t - win // 2, 0, L)
        hi = np.clip(t - win // 2 + win, 0, L)
        cnt = jnp.asarray(hi - lo, jnp.float32)[None, :, None]
        outs.append((c[:, hi, g] - c[:, lo, g]) / cnt - uf[:, :, g])
    pooled = jnp.stack(outs, axis=2)
    y = jnp.einsum('blgc,gcd->blgd', pooled, w.astype(jnp.float32)).reshape(B, L, POOL_WIDTH)
    return (y * scale.astype(jnp.float32)).astype(u.dtype)


def encoder_layer(x, rel_bias, norm_mix, w_in, q_gain, k_gain, attn_sink, hy_conv_w, hy_conv_b,
                  hy_w1, hy_b1, hy_w2, hy_b2, hy_w3, hy_freq, hy_skip, pool_w, pool_scale,
                  w_attn_o, w_hyena_o, w_pool_o, w_out, norm_ffn, w_gate, w_up, ffn_conv_w,
                  ffn_conv_b, w_down):
    B, L, _ = x.shape
    h = rmsnorm(x, norm_mix)
    proj = h @ w_in
    o1 = ATTN_WIDTH
    o2 = o1 + KV_WIDTH
    o3 = o2 + KV_WIDTH
    o4 = o3 + 3 * HYENA_WIDTH
    o5 = o4 + POOL_WIDTH
    q = proj[..., :o1].reshape(B, L, N_HEADS, HEAD_DIM)
    k = proj[..., o1:o2].reshape(B, L, N_KV_HEADS, HEAD_DIM)
    v = proj[..., o2:o3].reshape(B, L, N_KV_HEADS, HEAD_DIM)
    attn = window_attention(q, k, v, q_gain, k_gain, attn_sink, rel_bias)
    hy = hyena_mixer(proj[..., o3:o4], hy_conv_w, hy_conv_b, hy_w1, hy_b1, hy_w2, hy_b2,
                     hy_w3, hy_freq, hy_skip)
    po = pool_mixer(proj[..., o4:o5], pool_w, pool_scale)
    g = jax.nn.sigmoid(proj[..., o5:].astype(jnp.float32)).astype(x.dtype)
    g = g.reshape(B, L, N_BRANCH, D_MODEL)
    merged = (g[:, :, 0] * (attn @ w_attn_o) + g[:, :, 1] * (hy @ w_hyena_o)
              + g[:, :, 2] * (po @ w_pool_o))
    x = x + (merged @ w_out).astype(x.dtype)
    h = rmsnorm(x, norm_ffn)
    a = jax.nn.gelu(dwconv3(h @ w_gate, ffn_conv_w, ffn_conv_b), approximate=False)
    return x + ((a * (h @ w_up)) @ w_down).astype(x.dtype)


def run_trunk(x, meta_tokens, rel_bias, stacks):
    B = x.shape[0]
    meta = jnp.broadcast_to(meta_tokens.astype(x.dtype)[None], (B, N_META, D_MODEL))
    h = jnp.concatenate([meta, x], axis=1)
    for l in range(DEPTH):
        h = encoder_layer(h, rel_bias, *[p[l] for p in stacks])
    return h[:, N_META:]


def setup_inputs(seed: int = 0) -> dict:
    key = jax.random.key(seed)
    ks = jax.random.split(key, 32)
    nrm = lambda k, shape, s: jax.random.normal(k, shape, jnp.float32) * s
    gain = lambda k, shape: 1.0 + 0.02 * jax.random.normal(k, shape, jnp.float32)
    return {
        "x_prompt": nrm(ks[0], (BATCH, SEQ, D_MODEL), 1.0),
        "x_sample": nrm(ks[1], (DEC_BATCH, DEC_SEQ, D_MODEL), 1.0),
        "meta_tokens": nrm(ks[2], (N_META, D_MODEL), 1.0),
        "rel_bias": nrm(ks[3], (N_BUCKETS, N_HEADS), 0.2),
        "norm_mix": gain(ks[4], (DEPTH, D_MODEL)),
        "w_in": nrm(ks[5], (DEPTH, D_MODEL, IN_WIDTH), D_MODEL ** -0.5),
        "q_gain": gain(ks[6], (DEPTH, HEAD_DIM)),
        "k_gain": gain(ks[7], (DEPTH, HEAD_DIM)),
        "attn_sink": nrm(ks[8], (DEPTH, N_HEADS), 0.5),
        "hy_conv_w": nrm(ks[9], (DEPTH, 3, 3 * HYENA_WIDTH), 3 ** -0.5),
        "hy_conv_b": nrm(ks[10], (DEPTH, 3 * HYENA_WIDTH), 0.01),
        "hy_w1": nrm(ks[11], (DEPTH, HYENA_EMB, HYENA_FILTER_HIDDEN), HYENA_EMB ** -0.5),
        "hy_b1": nrm(ks[12], (DEPTH, HYENA_FILTER_HIDDEN), 0.01),
        "hy_w2": nrm(ks[13], (DEPTH, HYENA_FILTER_HIDDEN, HYENA_FILTER_HIDDEN), HYENA_FILTER_HIDDEN ** -0.5),
        "hy_b2": nrm(ks[14], (DEPTH, HYENA_FILTER_HIDDEN), 0.01),
        "hy_w3": nrm(ks[15], (DEPTH, HYENA_FILTER_HIDDEN, 2 * HYENA_WIDTH), HYENA_FILTER_HIDDEN ** -0.5),
        "hy_freq": gain(ks[16], (DEPTH, HYENA_FILTER_HIDDEN)),
        "hy_skip": nrm(ks[17], (DEPTH, HYENA_WIDTH), 1.0),
        "pool_w": nrm(ks[18], (DEPTH, POOL_GROUPS, POOL_GROUP, POOL_GROUP), POOL_GROUP ** -0.5),
        "pool_scale": gain(ks[19], (DEPTH, POOL_WIDTH)),
        "w_attn_o": nrm(ks[20], (DEPTH, ATTN_WIDTH, D_MODEL), ATTN_WIDTH ** -0.5),
        "w_hyena_o": nrm(ks[21], (DEPTH, HYENA_WIDTH, D_MODEL), HYENA_WIDTH ** -0.5),
        "w_pool_o": nrm(ks[22], (DEPTH, POOL_WIDTH, D_MODEL), POOL_WIDTH ** -0.5),
        "w_out": nrm(ks[23], (DEPTH, D_MODEL, D_MODEL), D_MODEL ** -0.5),
        "norm_ffn": gain(ks[24], (DEPTH, D_MODEL)),
        "w_gate": nrm(ks[25], (DEPTH, D_MODEL, D_FF), D_MODEL ** -0.5),
        "w_up": nrm(ks[26], (DEPTH, D_MODEL, D_FF), D_MODEL ** -0.5),
        "ffn_conv_w": nrm(ks[27], (DEPTH, 3, D_FF), 3 ** -0.5),
        "ffn_conv_b": nrm(ks[28], (DEPTH, D_FF), 0.01),
        "w_down": nrm(ks[29], (DEPTH, D_FF, D_MODEL), D_FF ** -0.5),
    }


def reference(x_prompt, x_sample, meta_tokens, rel_bias, norm_mix, w_in, q_gain, k_gain, attn_sink,
              hy_conv_w, hy_conv_b, hy_w1, hy_b1, hy_w2, hy_b2, hy_w3, hy_freq, hy_skip, pool_w,
              pool_scale, w_attn_o, w_hyena_o, w_pool_o, w_out, norm_ffn, w_gate, w_up,
              ffn_conv_w, ffn_conv_b, w_down):
    stacks = (norm_mix, w_in, q_gain, k_gain, attn_sink, hy_conv_w, hy_conv_b, hy_w1, hy_b1,
              hy_w2, hy_b2, hy_w3, hy_freq, hy_skip, pool_w, pool_scale, w_attn_o, w_hyena_o,
              w_pool_o, w_out, norm_ffn, w_gate, w_up, ffn_conv_w, ffn_conv_b, w_down)
    y_prompt = run_trunk(x_prompt, meta_tokens, rel_bias, stacks)
    y_sample = run_trunk(x_sample, meta_tokens, rel_bias, stacks)
    return (y_prompt, y_sample)
```

```python
import functools
import math

import numpy as np
import jax
import jax.numpy as jnp
from jax import lax
from jax.experimental import pallas as pl
from jax.experimental.pallas import tpu as pltpu

F32 = jnp.float32
BF16 = jnp.bfloat16

D_MODEL = 1024
N_META = 16
N_HEADS = 8
N_KV = 2
HEAD_DIM = 64
ATTN_W = N_HEADS * HEAD_DIM
KV_W = N_KV * HEAD_DIM
WINDOW = 128
BLK = 128
N_BUCKETS = 32
MAX_DIST = 128
HY_W = 512
HY_EMB = 33
HY_HID = 64
HY_DECAY_TARGET = 1e-2
HY_FAST = 0.3
HY_SLOW = 1.5
POOL_W = 512
POOL_WINDOWS = (2, 4, 8, 16)
POOL_G = 128
D_FF = 2816
EPS = 1e-6
NEG = -1e30
HALO = 8
LANE = 128
VMEM_LIMIT = 56 << 20


def _cp(*sem):
    return pltpu.CompilerParams(dimension_semantics=sem, vmem_limit_bytes=VMEM_LIMIT)


def _round_up(a, m):
    return -(-a // m) * m


def _row_tile(rows, cap, mult=16):
    best = None
    for t in range(mult, min(rows, cap) + 1, mult):
        if rows % t == 0:
            best = t
    assert best is not None, (rows, cap)
    return best


def _const_spec(shape):
    nd = len(shape)
    return pl.BlockSpec(shape, lambda *_: (0,) * nd, pipeline_mode=pl.Buffered(1))


def _dot(a, b):
    return jnp.dot(a, b, preferred_element_type=F32)


def _split(a):
    hi = a.astype(BF16)
    lo = (a - hi.astype(F32)).astype(BF16)
    return hi, lo


def _rms(x, g):
    ms = jnp.mean(x * x, axis=-1, keepdims=True)
    return x * lax.rsqrt(ms + EPS) * g


def _head_meansq(a, e):
    hi, lo = _split(a * a)
    cols = []
    for j in range(a.shape[1] // LANE):
        sl = slice(j * LANE, (j + 1) * LANE)
        cols.append(_dot(hi[:, sl], e) + _dot(lo[:, sl], e))
    return jnp.concatenate(cols, axis=1) * (1.0 / HEAD_DIM)


def _inproj_kernel(x_ref, g_ref, wq_ref, wk_ref, wv_ref, wh_ref, wp_ref, qg_ref, kg_ref, e_ref,
                   q_ref, k_ref, v_ref, hy_ref, po_ref):
    h = _rms(x_ref[0], g_ref[...]).astype(BF16)
    e = e_ref[...]
    q = _dot(h, wq_ref[...])
    q_ref[0] = (q * lax.rsqrt(_head_meansq(q, e) + EPS) * qg_ref[...]).astype(BF16)
    k = _dot(h, wk_ref[...])
    k_ref[0] = (k * lax.rsqrt(_head_meansq(k, e) + EPS) * kg_ref[...]).astype(BF16)
    v_ref[0] = _dot(h, wv_ref[...]).astype(BF16)
    hy_ref[0] = _dot(h, wh_ref[...])
    po_ref[0] = _dot(h, wp_ref[...])


def _inproj(x, g, wq, wk, wv, wh, wp, qg, kg, e):
    B, Lp, _ = x.shape
    tm = _row_tile(Lp, 704)
    row = lambda w: pl.BlockSpec((1, tm, w), lambda b, i: (b, i, 0))
    return pl.pallas_call(
        _inproj_kernel,
        grid=(B, Lp // tm),
        in_specs=[row(D_MODEL), _const_spec(g.shape), _const_spec(wq.shape), _const_spec(wk.shape),
                  _const_spec(wv.shape), _const_spec(wh.shape), _const_spec(wp.shape),
                  _const_spec(qg.shape), _const_spec(kg.shape), _const_spec(e.shape)],
        out_specs=[row(ATTN_W), row(2 * KV_W), row(2 * KV_W), row(3 * HY_W), row(POOL_W)],
        out_shape=[jax.ShapeDtypeStruct((B, Lp, ATTN_W), BF16),
                   jax.ShapeDtypeStruct((B, Lp, 2 * KV_W), BF16),
                   jax.ShapeDtypeStruct((B, Lp, 2 * KV_W), BF16),
                   jax.ShapeDtypeStruct((B, Lp, 3 * HY_W), F32),
                   jax.ShapeDtypeStruct((B, Lp, POOL_W), F32)],
        compiler_params=_cp("parallel", "parallel"),
        name="inproj",
    )(x, g, wq, wk, wv, wh, wp, qg, kg, e)


def _attn_kernel(L, sink_ref, q_ref, kp_ref, ko_ref, kn_ref, vp_ref, vo_ref, vn_ref, bias_ref, o_ref):
    j = pl.program_id(1)
    q = q_ref[0]
    kwin = jnp.concatenate([kp_ref[0], ko_ref[0], kn_ref[0]], axis=0)
    vwin = jnp.concatenate([vp_ref[0], vo_ref[0], vn_ref[0]], axis=0)
    kpos = (j - 1) * BLK + lax.broadcasted_iota(jnp.int32, (1, 3 * BLK), 1)
    kvalid = (kpos >= 0) & (kpos < L)
    lane = lax.broadcasted_iota(jnp.int32, (BLK, LANE), 1)
    low = lane < HEAD_DIM
    zero = jnp.zeros((BLK, LANE), BF16)
    group = N_HEADS // N_KV
    outs = []
    for kv in range(N_KV):
        kk = kwin[:, kv * LANE:(kv + 1) * LANE]
        vv = vwin[:, kv * LANE:(kv + 1) * LANE]
        parts = []
        for c in range(2):
            qc = q[:, (2 * kv + c) * LANE:(2 * kv + c + 1) * LANE]
            parts += [jnp.where(low, qc, zero), jnp.where(low, zero, qc)]
        qm = jnp.concatenate(parts, axis=0)
        s = lax.dot_general(qm, kk, (((1,), (1,)), ((), ())), preferred_element_type=F32)
        s = s + bias_ref[kv * group * BLK:(kv + 1) * group * BLK, :]
        s = jnp.where(kvalid, s, NEG)
        sink = jnp.concatenate(
            [jnp.full((BLK, 1), sink_ref[kv * group + g], F32) for g in range(group)], axis=0)
        m = jnp.maximum(jnp.max(s, axis=-1, keepdims=True), sink)
        p = jnp.exp(s - m)
        denom = jnp.sum(p, axis=-1, keepdims=True) + jnp.exp(sink - m)
        o = _dot(p.astype(BF16), vv) / denom
        for c in range(2):
            outs.append(jnp.where(low, o[(2 * c) * BLK:(2 * c + 1) * BLK],
                                  o[(2 * c + 1) * BLK:(2 * c + 2) * BLK]))
    out = jnp.concatenate(outs, axis=1)
    qpos = j * BLK + lax.broadcasted_iota(jnp.int32, (BLK, 1), 0)
    o_ref[0] = jnp.where(qpos < L, out, 0.0).astype(BF16)


def _attention(q, k, v, bias, sink, L):
    B, Lp, _ = q.shape
    nb = Lp // BLK
    cur = lambda w: pl.BlockSpec((1, BLK, w), lambda b, j: (b, j, 0))
    prv = lambda w: pl.BlockSpec((1, BLK, w), lambda b, j: (b, jnp.maximum(j - 1, 0), 0))
    nxt = lambda w: pl.BlockSpec((1, BLK, w), lambda b, j: (b, jnp.minimum(j + 1, nb - 1), 0))
    kw = 2 * KV_W
    return pl.pallas_call(
        functools.partial(_attn_kernel, L),
        grid=(B, nb),
        in_specs=[pl.BlockSpec(memory_space=pltpu.SMEM), cur(ATTN_W), prv(kw), cur(kw), nxt(kw),
                  prv(kw), cur(kw), nxt(kw), _const_spec(bias.shape)],
        out_specs=cur(ATTN_W),
        out_shape=jax.ShapeDtypeStruct((B, Lp, ATTN_W), BF16),
        compiler_params=_cp("parallel", "parallel"),
        name="attention",
    )(sink, q, k, k, k, v, v, v, bias)


def _halo_specs(tm, rows, width):
    r8 = tm // HALO
    last = rows // HALO - 1
    main = pl.BlockSpec((1, tm, width), lambda b, i: (b, i, 0))
    prev = pl.BlockSpec((1, HALO, width), lambda b, i: (b, jnp.maximum(i * r8 - 1, 0), 0))
    nxt = pl.BlockSpec((1, HALO, width), lambda b, i: (b, jnp.minimum((i + 1) * r8, last), 0))
    return main, prev, nxt


def _with_halo(prev_ref, main_ref, next_ref, tile_start, tm, rows):
    pos = tile_start - HALO + lax.broadcasted_iota(jnp.int32, (tm + 2 * HALO, 1), 0)
    ext = jnp.concatenate([prev_ref[0], main_ref[0], next_ref[0]], axis=0)
    return jnp.where((pos >= 0) & (pos < rows), ext, 0.0)


def _shift(a, s):
    return pltpu.roll(a, s % a.shape[0], axis=0)


def _pool_kernel(L, Lp, tm, up_ref, u_ref, un_ref, w_ref, sc_ref, o_ref):
    i = pl.program_id(1)
    ext = _with_halo(up_ref, u_ref, un_ref, i * tm, tm, Lp)
    pos = i * tm + lax.broadcasted_iota(jnp.int32, (tm, 1), 0)
    outs = []
    for g, win in enumerate(POOL_WINDOWS):
        a = ext[:, g * POOL_G:(g + 1) * POOL_G]
        s = a + _shift(a, 1)
        half = 1
        while 2 * half < win:
            s = _shift(s, half) + _shift(s, -half)
            half *= 2
        lo = jnp.maximum(pos - win // 2, 0)
        hi = jnp.minimum(pos - win // 2 + win, L)
        cnt = jnp.maximum(hi - lo, 1).astype(F32)
        pooled = s[HALO:HALO + tm] / cnt - a[HALO:HALO + tm]
        pooled = jnp.where(pos < L, pooled, 0.0)
        outs.append(_dot(pooled.astype(BF16), w_ref[g]))
    o_ref[0] = (jnp.concatenate(outs, axis=1) * sc_ref[...]).astype(BF16)


def _pool(u, w, scale, L):
    B, Lp, _ = u.shape
    tm = _row_tile(Lp, 1408)
    main, prev, nxt = _halo_specs(tm, Lp, POOL_W)
    return pl.pallas_call(
        functools.partial(_pool_kernel, L, Lp, tm),
        grid=(B, Lp // tm),
        in_specs=[prev, main, nxt, _const_spec(w.shape), _const_spec(scale.shape)],
        out_specs=main,
        out_shape=jax.ShapeDtypeStruct((B, Lp, POOL_W), BF16),
        compiler_params=_cp("parallel", "parallel"),
        name="pool_mixer",
    )(u, u, u, w, scale)


def _hy_prep_kernel(L, Lp, tm, up_ref, u_ref, un_ref, cw_ref, cb_ref, z_ref, x0_ref):
    i = pl.program_id(1)
    ext = _with_halo(up_ref, u_ref, un_ref, i * tm, tm, Lp)
    cw = cw_ref[...]
    u = (cw[0:1] * _shift(ext, 1) + cw[1:2] * ext + cw[2:3] * _shift(ext, -1))[HALO:HALO + tm]
    u = u + cb_ref[...]
    pos = i * tm + lax.broadcasted_iota(jnp.int32, (tm, 1), 0)
    valid = pos < L
    z_ref[0] = jnp.where(valid, u[:, 2 * HY_W:] * u[:, HY_W:2 * HY_W], 0.0)
    x0_ref[0] = jnp.where(valid, u[:, :HY_W], 0.0)


def _hy_prep(u, cw, cb, L, nz):
    B, Lp, _ = u.shape
    tm = 1024
    rows = nz * BLK
    assert rows % tm == 0
    last_blk = pl.cdiv(Lp, tm) - 1
    last8 = Lp // HALO - 1
    r8 = tm // HALO
    main = pl.BlockSpec((1, tm, 3 * HY_W), lambda b, i: (b, jnp.minimum(i, last_blk), 0))
    prev = pl.BlockSpec((1, HALO, 3 * HY_W),
                        lambda b, i: (b, jnp.clip(i * r8 - 1, 0, last8), 0))
    nxt = pl.BlockSpec((1, HALO, 3 * HY_W),
                       lambda b, i: (b, jnp.minimum((i + 1) * r8, last8), 0))
    out = pl.BlockSpec((1, tm, HY_W), lambda b, i: (b, i, 0))
    return pl.pallas_call(
        functools.partial(_hy_prep_kernel, L, Lp, tm),
        grid=(B, rows // tm),
        in_specs=[prev, main, nxt, _const_spec(cw.shape), _const_spec(cb.shape)],
        out_specs=[out, out],
        out_shape=[jax.ShapeDtypeStruct((B, rows, HY_W), F32)] * 2,
        compiler_params=_cp("parallel", "parallel"),
        name="hyena_prep",
    )(u, u, u, cw, cb)


def _dot_hl(m_hi, m_lo, d):
    d_hi, d_lo = _split(d)
    return _dot(m_hi, d_hi) + _dot(m_hi, d_lo) + _dot(m_lo, d_hi)


def _fwd1_kernel(mh_ref, ml_ref, z_ref, a_ref):
    a_ref[0] = _dot_hl(mh_ref[0], ml_ref[0], z_ref[0])


def _fwd1(z, m1h, m1l):
    B, rows, _ = z.shape
    nz = rows // BLK
    two_n1p = m1h.shape[1]
    zv = z.reshape(B, nz, BLK * HY_W)
    a = pl.pallas_call(
        _fwd1_kernel,
        grid=(B, BLK),
        in_specs=[pl.BlockSpec((1, two_n1p, nz), lambda b, s: (s, 0, 0)),
                  pl.BlockSpec((1, two_n1p, nz), lambda b, s: (s, 0, 0)),
                  pl.BlockSpec((1, nz, HY_W), lambda b, s: (b, 0, s))],
        out_specs=pl.BlockSpec((1, two_n1p, HY_W), lambda b, s: (b, 0, s)),
        out_shape=jax.ShapeDtypeStruct((B, two_n1p, BLK * HY_W), F32),
        compiler_params=_cp("parallel", "parallel"),
        name="hyena_fwd1",
    )(m1h, m1l, zv)
    return a.reshape(B, two_n1p, BLK, HY_W)


def _mid_kernel(ffh_ref, ffl_ref, fih_ref, fil_ref, ar_ref, ai_ref, kr_ref, ki_ref, gr_ref, gi_ref):
    x = _dot_hl(ffh_ref[...], ffl_ref[...], jnp.concatenate([ar_ref[0, 0], ai_ref[0, 0]], axis=0))
    xr, xi = x[:BLK], x[BLK:]
    kr, ki = kr_ref[0], ki_ref[0]
    y = jnp.concatenate([xr * kr - xi * ki, xr * ki + xi * kr], axis=0)
    g = _dot_hl(fih_ref[...], fil_ref[...], y)
    gr_ref[0, 0] = g[:BLK]
    gi_ref[0, 0] = g[BLK:]


def _mid(a, kr, ki, f2):
    B, two_n1p, _, _ = a.shape
    n1p = two_n1p // 2
    ffh, ffl, fih, fil = f2
    blk = lambda off: pl.BlockSpec((1, 1, BLK, HY_W), lambda b, k: (b, k + off, 0, 0))
    kblk = pl.BlockSpec((1, BLK, HY_W), lambda b, k: (k, 0, 0))
    return pl.pallas_call(
        _mid_kernel,
        grid=(B, n1p),
        in_specs=[_const_spec(ffh.shape)] * 4 + [blk(0), blk(n1p), kblk, kblk],
        out_specs=[blk(0), blk(0)],
        out_shape=[jax.ShapeDtypeStruct((B, n1p, BLK, HY_W), F32)] * 2,
        compiler_params=_cp("parallel", "parallel"),
        name="hyena_mid",
    )(ffh, ffl, fih, fil, a, a, kr, ki)


def _inv1_kernel(mh_ref, ml_ref, gr_ref, gi_ref, z_ref, x0_ref, skip_ref, o_ref):
    g = jnp.concatenate([gr_ref[0], gi_ref[0]], axis=0)
    y = _dot_hl(mh_ref[0], ml_ref[0], g)
    o_ref[0] = ((y + z_ref[0] * skip_ref[...]) * x0_ref[0]).astype(BF16)


def _inv1(gr, gi, z, x0, skip, mih, mil):
    B, n1p, _, _ = gr.shape
    rows = z.shape[1]
    nz = rows // BLK
    wide = BLK * HY_W
    col = lambda r: pl.BlockSpec((1, r, HY_W), lambda b, s: (b, 0, s))
    mat = pl.BlockSpec((1, nz, 2 * n1p), lambda b, s: (s, 0, 0))
    out = pl.pallas_call(
        _inv1_kernel,
        grid=(B, BLK),
        in_specs=[mat, mat, col(n1p), col(n1p), col(nz), col(nz), _const_spec(skip.shape)],
        out_specs=col(nz),
        out_shape=jax.ShapeDtypeStruct((B, nz, wide), BF16),
        compiler_params=_cp("parallel", "parallel"),
        name="hyena_inv1",
    )(mih, mil, gr.reshape(B, n1p, wide), gi.reshape(B, n1p, wide),
      z.reshape(B, nz, wide), x0.reshape(B, nz, wide), skip)
    return out.reshape(B, rows, HY_W)


def _filter_kernel(L, tm, w1_ref, b1_ref, w2_ref, b2_ref, w3_ref, fr_ref, dl_ref, k_ref, sum_ref):
    i = pl.program_id(0)
    m = i * tm + lax.broadcasted_iota(jnp.int32, (tm, 1), 0)
    mf = m.astype(F32)
    t = mf * (1.0 / (L - 1))
    wpos = mf * (2.0 * math.pi / L)
    bands = (HY_EMB - 1) // 2
    lane = lax.broadcasted_iota(jnp.int32, (1, LANE), 1)
    band = jnp.where(lane <= bands, lane - 1, lane - 1 - bands)
    f = 1e-4 + ((bands - 1 - 1e-4) / (bands - 1)) * band.astype(F32)
    ang = f * wpos
    feat = jnp.where(lane == 0, t,
                     jnp.where(lane <= bands, jnp.cos(ang),
                               jnp.where(lane < HY_EMB, -jnp.sin(ang), 0.0)))
    hp = lax.Precision.HIGHEST
    dot = lambda a, b: jnp.dot(a, b, precision=hp, preferred_element_type=F32)
    fr = fr_ref[...]
    h = jnp.sin(fr * (dot(feat, w1_ref[...]) + b1_ref[...]))
    h = jnp.sin(fr * (dot(h, w2_ref[...]) + b2_ref[...]))
    kf = dot(h, w3_ref[...])
    decay = jnp.exp(-t * dl_ref[...])
    decay2 = jnp.concatenate([decay, decay], axis=1)
    col = lax.broadcasted_iota(jnp.int32, (1, 2 * HY_W), 1)
    keep = (m < L) & ((col < HY_W) | (m >= 1))
    k = jnp.where(keep, kf * decay2, 0.0)
    k_ref[...] = k

    @pl.when(i == 0)
    def _():
        sum_ref[...] = jnp.zeros_like(sum_ref)

    sum_ref[...] += jnp.sum(jnp.abs(k).reshape(tm // 8, 8, 2 * HY_W), axis=0)


def _filter_taps(L, nz, w1, b1, w2, b2, w3, freq, deltas):
    rows = nz * BLK
    tm = _row_tile(rows, 1024, 8)
    args = (w1, b1, w2, b2, w3, freq, deltas)
    return pl.pallas_call(
        functools.partial(_filter_kernel, L, tm),
        grid=(rows // tm,),
        in_specs=[_const_spec(a.shape) for a in args],
        out_specs=[pl.BlockSpec((tm, 2 * HY_W), lambda i: (i, 0)),
                   pl.BlockSpec((8, 2 * HY_W), lambda i: (0, 0))],
        out_shape=[jax.ShapeDtypeStruct((rows, 2 * HY_W), F32),
                   jax.ShapeDtypeStruct((8, 2 * HY_W), F32)],
        compiler_params=_cp("arbitrary"),
        name="hyena_filter",
    )(*args)


def _kfwd1_kernel(n1p, mh_ref, ml_ref, mch_ref, mcl_ref, kf_ref, kb_ref, a_ref):
    fwd = _dot_hl(mh_ref[0], ml_ref[0], kf_ref[...])
    bwd = _dot_hl(mch_ref[0], mcl_ref[0], kb_ref[...])
    a_ref[...] = fwd + jnp.concatenate([bwd[:n1p], -bwd[n1p:]], axis=0)


def _kfwd1(km, m1h, m1l):
    rows = km.shape[0]
    nz = rows // BLK
    two_n1p = m1h.shape[1]
    kv = km.reshape(nz, BLK * 2 * HY_W)
    mat = lambda f: pl.BlockSpec((1, two_n1p, nz), lambda s: (f(s), 0, 0))
    same = lambda s: s
    mirror = lambda s: (BLK - s) % BLK
    a = pl.pallas_call(
        functools.partial(_kfwd1_kernel, two_n1p // 2),
        grid=(BLK,),
        in_specs=[mat(same), mat(same), mat(mirror), mat(mirror),
                  pl.BlockSpec((nz, HY_W), lambda s: (0, 2 * s)),
                  pl.BlockSpec((nz, HY_W), lambda s: (0, 2 * mirror(s) + 1))],
        out_specs=pl.BlockSpec((two_n1p, HY_W), lambda s: (0, s)),
        out_shape=jax.ShapeDtypeStruct((two_n1p, BLK * HY_W), F32),
        compiler_params=_cp("parallel"),
        name="hyena_filter_fwd1",
    )(m1h, m1l, m1h, m1l, kv, kv)
    return a.reshape(two_n1p, BLK, HY_W)


def _kmid_kernel(ffh_ref, ffl_ref, ar_ref, ai_ref, sum_ref, kr_ref, ki_ref):
    x = _dot_hl(ffh_ref[...], ffl_ref[...], jnp.concatenate([ar_ref[0], ai_ref[0]], axis=0))
    s = jnp.sum(sum_ref[...], axis=0, keepdims=True)
    inv = 1.0 / (s[:, :HY_W] + s[:, HY_W:])
    kr_ref[0] = x[:BLK] * inv
    ki_ref[0] = x[BLK:] * inv


def _kmid(a, ksum, f2):
    two_n1p = a.shape[0]
    n1p = two_n1p // 2
    ffh, ffl = f2[0], f2[1]
    blk = lambda off: pl.BlockSpec((1, BLK, HY_W), lambda k: (k + off, 0, 0))
    return pl.pallas_call(
        _kmid_kernel,
        grid=(n1p,),
        in_specs=[_const_spec(ffh.shape)] * 2 + [blk(0), blk(n1p), _const_spec(ksum.shape)],
        out_specs=[blk(0), blk(0)],
        out_shape=[jax.ShapeDtypeStruct((n1p, BLK, HY_W), F32)] * 2,
        compiler_params=_cp("parallel"),
        name="hyena_filter_mid",
    )(ffh, ffl, a, a, ksum)


def _merge_kernel(x_ref, at_ref, hy_ref, po_ref, g_ref, wg_ref, wa_ref, wh_ref, wp_ref, wo_ref, o_ref):
    x = x_ref[0]
    h = _rms(x, g_ref[...]).astype(BF16)
    merged = None
    for n, (br_ref, w_ref) in enumerate(((at_ref, wa_ref), (hy_ref, wh_ref), (po_ref, wp_ref))):
        gate = jax.nn.sigmoid(_dot(h, wg_ref[:, n * D_MODEL:(n + 1) * D_MODEL]))
        term = gate * _dot(br_ref[0], w_ref[...])
        merged = term if merged is None else merged + term
    o_ref[0] = x + _dot(merged.astype(BF16), wo_ref[...])


def _merge(x, attn, hy, po, g, wg, wa, wh, wp, wo):
    B, Lp, _ = x.shape
    tm = _row_tile(Lp, 704)
    row = lambda w: pl.BlockSpec((1, tm, w), lambda b, i: (b, i, 0))
    weights = (g, wg, wa, wh, wp, wo)
    return pl.pallas_call(
        _merge_kernel,
        grid=(B, Lp // tm),
        in_specs=[row(D_MODEL), row(ATTN_W), row(HY_W), row(POOL_W)]
                 + [_const_spec(w.shape) for w in weights],
        out_specs=row(D_MODEL),
        out_shape=jax.ShapeDtypeStruct((B, Lp, D_MODEL), F32),
        compiler_params=_cp("parallel", "parallel"),
        name="merge",
    )(x, attn, hy, po, *weights)


def _ffn_kernel(Lp, tm, fc, xp_ref, x_ref, xn_ref, g_ref, wg_ref, wu_ref, cw_ref, cb_ref, wd_ref, o_ref):
    i = pl.program_id(1)
    ext = _with_halo(xp_ref, x_ref, xn_ref, i * tm, tm, Lp)
    h = _rms(ext, g_ref[...]).astype(BF16)
    hm = h[HALO:HALO + tm]
    acc = x_ref[0]
    for c in range(D_FF // fc):
        sl = slice(c * fc, (c + 1) * fc)
        gp = _dot(h, wg_ref[:, sl])
        cw = cw_ref[:, sl]
        conv = (cw[0:1] * _shift(gp, 1) + cw[1:2] * gp + cw[2:3] * _shift(gp, -1))[HALO:HALO + tm]
        conv = conv + cb_ref[:, sl]
        act = 0.5 * conv * (1.0 + lax.erf(conv * (1.0 / math.sqrt(2.0))))
        up = _dot(hm, wu_ref[:, sl])
        acc = acc + _dot((act * up).astype(BF16), wd_ref[sl, :])
    o_ref[0] = acc


def _ffn(x, g, wg, wu, cw, cb, wd):
    B, Lp, _ = x.shape
    tm = _row_tile(Lp, 704)
    fc = 1408
    main, prev, nxt = _halo_specs(tm, Lp, D_MODEL)
    weights = (g, wg, wu, cw, cb, wd)
    return pl.pallas_call(
        functools.partial(_ffn_kernel, Lp, tm, fc),
        grid=(B, Lp // tm),
        in_specs=[prev, main, nxt] + [_const_spec(w.shape) for w in weights],
        out_specs=main,
        out_shape=jax.ShapeDtypeStruct((B, Lp, D_MODEL), F32),
        compiler_params=_cp("parallel", "parallel"),
        name="ffn",
    )(x, x, x, *weights)


def _t5_buckets(rel):
    half = N_BUCKETS // 2
    ret = (rel > 0).astype(np.int32) * half
    n = np.abs(rel)
    max_exact = half // 2
    large = max_exact + (np.log(np.maximum(n, 1) / max_exact) / np.log(MAX_DIST / max_exact)
                         * (half - max_exact)).astype(np.int32)
    large = np.minimum(large, half - 1)
    return (ret + np.where(n < max_exact, n, large)).astype(np.int32)


def _bias_table(rel_bias):
    rel = np.arange(3 * BLK)[None, :] - BLK - np.arange(BLK)[:, None]
    band = np.abs(rel) <= WINDOW
    bias = rel_bias.astype(F32)[_t5_buckets(rel)]
    bias = jnp.where(band[:, :, None], bias, NEG)
    return jnp.transpose(bias, (2, 0, 1)).reshape(N_HEADS * BLK, 3 * BLK)


def _dft_tables(n1, n1p, nz):
    n = n1 * BLK
    k1 = lax.broadcasted_iota(jnp.int32, (BLK, n1p, nz), 1)
    s = lax.broadcasted_iota(jnp.int32, (BLK, n1p, nz), 2) * BLK + lax.broadcasted_iota(
        jnp.int32, (BLK, n1p, nz), 0)
    ang = ((k1 * s) % n).astype(F32) * (2.0 * math.pi / n)
    live = k1 < n1
    m1 = jnp.concatenate([jnp.where(live, jnp.cos(ang), 0.0), jnp.where(live, -jnp.sin(ang), 0.0)],
                         axis=1)
    minv = jnp.swapaxes(m1, 1, 2) * (1.0 / n)
    k2 = np.arange(BLK)
    a2 = 2.0 * np.pi * ((k2[:, None] * k2[None, :]) % BLK) / BLK
    fr, fi = np.cos(a2), -np.sin(a2)
    ff = jnp.asarray(np.block([[fr, -fi], [fi, fr]]), F32)
    fv = jnp.asarray(np.block([[fr, fi], [-fi, fr]]), F32)
    return _split(m1), _split(minv), _split(ff) + _split(fv)


def _dup_heads(w):
    d = w.shape[0]
    w = w.reshape(d, N_KV, 1, HEAD_DIM)
    return jnp.broadcast_to(w, (d, N_KV, 2, HEAD_DIM)).reshape(d, 2 * KV_W)


def _geometry(L):
    Lp = _round_up(L + HALO, BLK)
    nz = _round_up(Lp // BLK, 16)
    n1 = _round_up(pl.cdiv(2 * L - 1, BLK), 2)
    n1p = _round_up(n1, 8)
    return Lp, nz, n1, n1p


def _layer(x, L, geom, tabs, bias, p):
    Lp, nz, n1, n1p = geom
    (m1h, m1l), (mih, mil), f2 = tabs
    q, k, v, u_hy, u_po = _inproj(x, p["norm_mix"], p["wq"], p["wk"], p["wv"], p["wh"], p["wp"],
                                  p["q_gain"], p["k_gain"], p["e"])
    attn = _attention(q, k, v, bias, p["sink"], L)
    po = _pool(u_po, p["pool_w"], p["pool_scale"], L)

    km, ksum = _filter_taps(L, nz, p["hy_w1"], p["hy_b1"], p["hy_w2"], p["hy_b2"], p["hy_w3"],
                            p["hy_freq"], p["deltas"])
    kr, ki = _kmid(_kfwd1(km, m1h, m1l), ksum, f2)
    z, x0 = _hy_prep(u_hy, p["hy_conv_w"], p["hy_conv_b"], L, nz)
    gr, gi = _mid(_fwd1(z, m1h, m1l), kr, ki, f2)
    hy = _inv1(gr, gi, z, x0, p["hy_skip"], mih, mil)

    x = _merge(x, attn, hy, po, p["norm_mix"], p["wgates"], p["w_attn_o"], p["w_hyena_o"],
               p["w_pool_o"], p["w_out"])
    return _ffn(x, p["norm_ffn"], p["w_gate"], p["w_up"], p["ffn_conv_w"], p["ffn_conv_b"],
                p["w_down"])


def _trunk(x, meta, bias, layers):
    B, S, _ = x.shape
    L = S + N_META
    geom = _geometry(L)
    Lp, nz, n1, n1p = geom
    tabs = _dft_tables(n1, n1p, nz)
    h = jnp.concatenate([jnp.broadcast_to(meta[None], (B, N_META, D_MODEL)), x,
                         jnp.zeros((B, Lp - L, D_MODEL), F32)], axis=1)
    for p in layers:
        h = _layer(h, L, geom, tabs, bias, p)
    return h[:, N_META:L]


def _prep_layers(norm_mix, w_in, q_gain, k_gain, attn_sink, hy_conv_w, hy_conv_b, hy_w1, hy_b1,
                 hy_w2, hy_b2, hy_w3, hy_freq, hy_skip, pool_w, pool_scale, w_attn_o, w_hyena_o,
                 w_pool_o, w_out, norm_ffn, w_gate, w_up, ffn_conv_w, ffn_conv_b, w_down):
    o1 = ATTN_W
    o2 = o1 + KV_W
    o3 = o2 + KV_W
    o4 = o3 + 3 * HY_W
    o5 = o4 + POOL_W
    lane = np.arange(LANE)
    e = jnp.asarray((lane[:, None] // HEAD_DIM) == (lane[None, :] // HEAD_DIM), BF16)
    deltas = jnp.abs(jnp.linspace(math.log(HY_DECAY_TARGET) / HY_FAST,
                                  math.log(HY_DECAY_TARGET) / HY_SLOW, HY_W, dtype=F32))[None]
    row = lambda a: a.astype(F32)[None]
    pad = lambda a, r, c: jnp.pad(a.astype(F32), ((0, r - a.shape[0]), (0, c - a.shape[1])))
    layers = []
    for l in range(norm_mix.shape[0]):
        w = w_in[l]
        layers.append(dict(
            norm_mix=row(norm_mix[l]),
            wq=w[:, :o1].astype(BF16),
            wk=_dup_heads(w[:, o1:o2]).astype(BF16),
            wv=_dup_heads(w[:, o2:o3]).astype(BF16),
            wh=w[:, o3:o4].astype(BF16),
            wp=w[:, o4:o5].astype(BF16),
            wgates=w[:, o5:].astype(BF16),
            q_gain=jnp.tile(row(q_gain[l]), (1, N_HEADS)) * (HEAD_DIM ** -0.5),
            k_gain=jnp.tile(row(k_gain[l]), (1, 2 * N_KV)),
            e=e,
            sink=attn_sink[l].astype(F32),
            hy_conv_w=hy_conv_w[l].astype(F32), hy_conv_b=row(hy_conv_b[l]),
            hy_w1=pad(hy_w1[l], LANE, LANE), hy_b1=pad(row(hy_b1[l]), 1, LANE),
            hy_w2=pad(hy_w2[l], LANE, LANE), hy_b2=pad(row(hy_b2[l]), 1, LANE),
            hy_w3=pad(hy_w3[l], LANE, 2 * HY_W), hy_freq=pad(row(hy_freq[l]), 1, LANE),
            hy_skip=row(hy_skip[l]), deltas=deltas,
            pool_w=pool_w[l].astype(BF16), pool_scale=row(pool_scale[l]),
            w_attn_o=w_attn_o[l].astype(BF16), w_hyena_o=w_hyena_o[l].astype(BF16),
            w_pool_o=w_pool_o[l].astype(BF16), w_out=w_out[l].astype(BF16),
            norm_ffn=row(norm_ffn[l]), w_gate=w_gate[l].astype(BF16), w_up=w_up[l].astype(BF16),
            ffn_conv_w=ffn_conv_w[l].astype(F32), ffn_conv_b=row(ffn_conv_b[l]),
            w_down=w_down[l].astype(BF16),
        ))
    return layers


def kernel(x_prompt, x_sample, meta_tokens, rel_bias, norm_mix, w_in, q_gain, k_gain, attn_sink,
           hy_conv_w, hy_conv_b, hy_w1, hy_b1, hy_w2, hy_b2, hy_w3, hy_freq, hy_skip, pool_w,
           pool_scale, w_attn_o, w_hyena_o, w_pool_o, w_out, norm_ffn, w_gate, w_up,
           ffn_conv_w, ffn_conv_b, w_down):
    layers = _prep_layers(norm_mix, w_in, q_gain, k_gain, attn_sink, hy_conv_w, hy_conv_b, hy_w1,
                          hy_b1, hy_w2, hy_b2, hy_w3, hy_freq, hy_skip, pool_w, pool_scale,
                          w_attn_o, w_hyena_o, w_pool_o, w_out, norm_ffn, w_gate, w_up,
                          ffn_conv_w, ffn_conv_b, w_down)
    bias = _bias_table(rel_bias)
    meta = meta_tokens.astype(F32)
    return (_trunk(x_prompt, meta, bias, layers), _trunk(x_sample, meta, bias, layers))
```

```python
import functools
import math

import numpy as np
import jax
import jax.numpy as jnp
from jax import lax
from jax.experimental import pallas as pl
from jax.experimental.pallas import tpu as pltpu

F32 = jnp.float32
BF16 = jnp.bfloat16

D_MODEL = 1024
N_META = 16
N_HEADS = 8
N_KV = 2
HEAD_DIM = 64
ATTN_W = N_HEADS * HEAD_DIM
KV_W = N_KV * HEAD_DIM
WINDOW = 128
BLK = 128
N_BUCKETS = 32
MAX_DIST = 128
HY_W = 512
HY_EMB = 33
HY_HID = 64
HY_DECAY_TARGET = 1e-2
HY_FAST = 0.3
HY_SLOW = 1.5
POOL_W = 512
POOL_WINDOWS = (2, 4, 8, 16)
POOL_G = 128
D_FF = 2816
EPS = 1e-6
NEG = -1e30
HALO = 8
LANE = 128
VMEM_LIMIT = 56 << 20


def _cp(*sem):
    return pltpu.CompilerParams(dimension_semantics=sem, vmem_limit_bytes=VMEM_LIMIT)


def _round_up(a, m):
    return -(-a // m) * m


def _row_tile(rows, cap, mult=16):
    best = None
    for t in range(mult, min(rows, cap) + 1, mult):
        if rows % t == 0:
            best = t
    assert best is not None, (rows, cap)
    return best


def _const_spec(shape):
    nd = len(shape)
    return pl.BlockSpec(shape, lambda *_: (0,) * nd, pipeline_mode=pl.Buffered(1))


def _dot(a, b):
    return jnp.dot(a, b, preferred_element_type=F32)


def _split(a):
    hi = a.astype(BF16)
    lo = (a - hi.astype(F32)).astype(BF16)
    return hi, lo


def _rms(x, g):
    ms = jnp.mean(x * x, axis=-1, keepdims=True)
    return x * lax.rsqrt(ms + EPS) * g


def _head_meansq(a, e):
    hi, lo = _split(a * a)
    cols = []
    for j in range(a.shape[1] // LANE):
        sl = slice(j * LANE, (j + 1) * LANE)
        cols.append(_dot(hi[:, sl], e) + _dot(lo[:, sl], e))
    return jnp.concatenate(cols, axis=1) * (1.0 / HEAD_DIM)


def _inproj_kernel(x_ref, g_ref, wq_ref, wk_ref, wv_ref, wh_ref, wp_ref, qg_ref, kg_ref, e_ref,
                   q_ref, k_ref, v_ref, hy_ref, po_ref):
    h = _rms(x_ref[0], g_ref[...]).astype(BF16)
    e = e_ref[...]
    q = _dot(h, wq_ref[...])
    q_ref[0] = (q * lax.rsqrt(_head_meansq(q, e) + EPS) * qg_ref[...]).astype(BF16)
    k = _dot(h, wk_ref[...])
    k_ref[0] = (k * lax.rsqrt(_head_meansq(k, e) + EPS) * kg_ref[...]).astype(BF16)
    v_ref[0] = _dot(h, wv_ref[...]).astype(BF16)
    hy_ref[0] = _dot(h, wh_ref[...])
    po_ref[0] = _dot(h, wp_ref[...])


def _inproj(x, g, wq, wk, wv, wh, wp, qg, kg, e):
    B, Lp, _ = x.shape
    tm = _row_tile(Lp, 704)
    row = lambda w: pl.BlockSpec((1, tm, w), lambda b, i: (b, i, 0))
    return pl.pallas_call(
        _inproj_kernel,
        grid=(B, Lp // tm),
        in_specs=[row(D_MODEL), _const_spec(g.shape), _const_spec(wq.shape), _const_spec(wk.shape),
                  _const_spec(wv.shape), _const_spec(wh.shape), _const_spec(wp.shape),
                  _const_spec(qg.shape), _const_spec(kg.shape), _const_spec(e.shape)],
        out_specs=[row(ATTN_W), row(2 * KV_W), row(2 * KV_W), row(3 * HY_W), row(POOL_W)],
        out_shape=[jax.ShapeDtypeStruct((B, Lp, ATTN_W), BF16),
                   jax.ShapeDtypeStruct((B, Lp, 2 * KV_W), BF16),
                   jax.ShapeDtypeStruct((B, Lp, 2 * KV_W), BF16),
                   jax.ShapeDtypeStruct((B, Lp, 3 * HY_W), F32),
                   jax.ShapeDtypeStruct((B, Lp, POOL_W), F32)],
        compiler_params=_cp("parallel", "parallel"),
        name="inproj",
    )(x, g, wq, wk, wv, wh, wp, qg, kg, e)


def _attn_kernel(L, sink_ref, q_ref, kp_ref, ko_ref, kn_ref, vp_ref, vo_ref, vn_ref, bias_ref, o_ref):
    j = pl.program_id(1)
    q = q_ref[0]
    kwin = jnp.concatenate([kp_ref[0], ko_ref[0], kn_ref[0]], axis=0)
    vwin = jnp.concatenate([vp_ref[0], vo_ref[0], vn_ref[0]], axis=0)
    kpos = (j - 1) * BLK + lax.broadcasted_iota(jnp.int32, (1, 3 * BLK), 1)
    kvalid = (kpos >= 0) & (kpos < L)
    lane = lax.broadcasted_iota(jnp.int32, (BLK, LANE), 1)
    low = lane < HEAD_DIM
    zero = jnp.zeros((BLK, LANE), BF16)
    group = N_HEADS // N_KV
    outs = []
    for kv in range(N_KV):
        kk = kwin[:, kv * LANE:(kv + 1) * LANE]
        vv = vwin[:, kv * LANE:(kv + 1) * LANE]
        parts = []
        for c in range(2):
            qc = q[:, (2 * kv + c) * LANE:(2 * kv + c + 1) * LANE]
            parts += [jnp.where(low, qc, zero), jnp.where(low, zero, qc)]
        qm = jnp.concatenate(parts, axis=0)
        s = lax.dot_general(qm, kk, (((1,), (1,)), ((), ())), preferred_element_type=F32)
        s = s + bias_ref[kv * group * BLK:(kv + 1) * group * BLK, :]
        s = jnp.where(kvalid, s, NEG)
        sink = jnp.concatenate(
            [jnp.full((BLK, 1), sink_ref[kv * group + g], F32) for g in range(group)], axis=0)
        m = jnp.maximum(jnp.max(s, axis=-1, keepdims=True), sink)
        p = jnp.exp(s - m)
        denom = jnp.sum(p, axis=-1, keepdims=True) + jnp.exp(sink - m)
        o = _dot(p.astype(BF16), vv) / denom
        for c in range(2):
            outs.append(jnp.where(low, o[(2 * c) * BLK:(2 * c + 1) * BLK],
                                  o[(2 * c + 1) * BLK:(2 * c + 2) * BLK]))
    out = jnp.concatenate(outs, axis=1)
    qpos = j * BLK + lax.broadcasted_iota(jnp.int32, (BLK, 1), 0)
    o_ref[0] = jnp.where(qpos < L, out, 0.0).astype(BF16)


def _attention(q, k, v, bias, sink, L):
    B, Lp, _ = q.shape
    nb = Lp // BLK
    cur = lambda w: pl.BlockSpec((1, BLK, w), lambda b, j: (b, j, 0))
    prv = lambda w: pl.BlockSpec((1, BLK, w), lambda b, j: (b, jnp.maximum(j - 1, 0), 0))
    nxt = lambda w: pl.BlockSpec((1, BLK, w), lambda b, j: (b, jnp.minimum(j + 1, nb - 1), 0))
    kw = 2 * KV_W
    return pl.pallas_call(
        functools.partial(_attn_kernel, L),
        grid=(B, nb),
        in_specs=[pl.BlockSpec(memory_space=pltpu.SMEM), cur(ATTN_W), prv(kw), cur(kw), nxt(kw),
                  prv(kw), cur(kw), nxt(kw), _const_spec(bias.shape)],
        out_specs=cur(ATTN_W),
        out_shape=jax.ShapeDtypeStruct((B, Lp, ATTN_W), BF16),
        compiler_params=_cp("parallel", "parallel"),
        name="attention",
    )(sink, q, k, k, k, v, v, v, bias)


def _halo_specs(tm, rows, width):
    r8 = tm // HALO
    last = rows // HALO - 1
    main = pl.BlockSpec((1, tm, width), lambda b, i: (b, i, 0))
    prev = pl.BlockSpec((1, HALO, width), lambda b, i: (b, jnp.maximum(i * r8 - 1, 0), 0))
    nxt = pl.BlockSpec((1, HALO, width), lambda b, i: (b, jnp.minimum((i + 1) * r8, last), 0))
    return main, prev, nxt


def _with_halo(prev_ref, main_ref, next_ref, tile_start, tm, rows):
    pos = tile_start - HALO + lax.broadcasted_iota(jnp.int32, (tm + 2 * HALO, 1), 0)
    ext = jnp.concatenate([prev_ref[0], main_ref[0], next_ref[0]], axis=0)
    return jnp.where((pos >= 0) & (pos < rows), ext, 0.0)


def _shift(a, s):
    return pltpu.roll(a, s % a.shape[0], axis=0)


def _pool_kernel(L, Lp, tm, up_ref, u_ref, un_ref, w_ref, sc_ref, o_ref):
    i = pl.program_id(1)
    ext = _with_halo(up_ref, u_ref, un_ref, i * tm, tm, Lp)
    pos = i * tm + lax.broadcasted_iota(jnp.int32, (tm, 1), 0)
    outs = []
    for g, win in enumerate(POOL_WINDOWS):
        a = ext[:, g * POOL_G:(g + 1) * POOL_G]
        s = a + _shift(a, 1)
        half = 1
        while 2 * half < win:
            s = _shift(s, half) + _shift(s, -half)
            half *= 2
        lo = jnp.maximum(pos - win // 2, 0)
        hi = jnp.minimum(pos - win // 2 + win, L)
        cnt = jnp.maximum(hi - lo, 1).astype(F32)
        pooled = s[HALO:HALO + tm] / cnt - a[HALO:HALO + tm]
        pooled = jnp.where(pos < L, pooled, 0.0)
        outs.append(_dot(pooled.astype(BF16), w_ref[g]))
    o_ref[0] = (jnp.concatenate(outs, axis=1) * sc_ref[...]).astype(BF16)


def _pool(u, w, scale, L):
    B, Lp, _ = u.shape
    tm = _row_tile(Lp, 1408)
    main, prev, nxt = _halo_specs(tm, Lp, POOL_W)
    return pl.pallas_call(
        functools.partial(_pool_kernel, L, Lp, tm),
        grid=(B, Lp // tm),
        in_specs=[prev, main, nxt, _const_spec(w.shape), _const_spec(scale.shape)],
        out_specs=main,
        out_shape=jax.ShapeDtypeStruct((B, Lp, POOL_W), BF16),
        compiler_params=_cp("parallel", "parallel"),
        name="pool_mixer",
    )(u, u, u, w, scale)


def _hy_prep_kernel(L, Lp, tm, up_ref, u_ref, un_ref, cw_ref, cb_ref, z_ref, x0_ref):
    i = pl.program_id(1)
    ext = _with_halo(up_ref, u_ref, un_ref, i * tm, tm, Lp)
    cw = cw_ref[...]
    u = (cw[0:1] * _shift(ext, 1) + cw[1:2] * ext + cw[2:3] * _shift(ext, -1))[HALO:HALO + tm]
    u = u + cb_ref[...]
    pos = i * tm + lax.broadcasted_iota(jnp.int32, (tm, 1), 0)
    valid = pos < L
    z_ref[0] = jnp.where(valid, u[:, 2 * HY_W:] * u[:, HY_W:2 * HY_W], 0.0)
    x0_ref[0] = jnp.where(valid, u[:, :HY_W], 0.0)


def _hy_prep(u, cw, cb, L, nz):
    B, Lp, _ = u.shape
    tm = 1024
    rows = nz * BLK
    assert rows % tm == 0
    last_blk = pl.cdiv(Lp, tm) - 1
    last8 = Lp // HALO - 1
    r8 = tm // HALO
    main = pl.BlockSpec((1, tm, 3 * HY_W), lambda b, i: (b, jnp.minimum(i, last_blk), 0))
    prev = pl.BlockSpec((1, HALO, 3 * HY_W),
                        lambda b, i: (b, jnp.clip(i * r8 - 1, 0, last8), 0))
    nxt = pl.BlockSpec((1, HALO, 3 * HY_W),
                       lambda b, i: (b, jnp.minimum((i + 1) * r8, last8), 0))
    out = pl.BlockSpec((1, tm, HY_W), lambda b, i: (b, i, 0))
    return pl.pallas_call(
        functools.partial(_hy_prep_kernel, L, Lp, tm),
        grid=(B, rows // tm),
        in_specs=[prev, main, nxt, _const_spec(cw.shape), _const_spec(cb.shape)],
        out_specs=[out, out],
        out_shape=[jax.ShapeDtypeStruct((B, rows, HY_W), F32)] * 2,
        compiler_params=_cp("parallel", "parallel"),
        name="hyena_prep",
    )(u, u, u, cw, cb)


def _dot_hl(m_hi, m_lo, d):
    d_hi, d_lo = _split(d)
    return _dot(m_hi, d_hi) + _dot(m_hi, d_lo) + _dot(m_lo, d_hi)


S2B = 16
FS2B = 8
KB = 2


def _lane_chunk_scratch(rows):
    return pltpu.VMEM((HY_W // LANE, rows, LANE), F32)


def _fill_chunks(scr, load_chunk):
    for c in range(HY_W // LANE):
        scr[c] = load_chunk(slice(c * LANE, (c + 1) * LANE)).astype(F32)


def _rows_at(scr, j, count, stride):
    return jnp.concatenate([scr[c, pl.ds(j, count, stride=stride), :]
                            for c in range(HY_W // LANE)], axis=1)


def _set_rows_at(scr, j, stride, val):
    for c in range(HY_W // LANE):
        scr[c, pl.ds(j, val.shape[0], stride=stride), :] = val[:, c * LANE:(c + 1) * LANE]


def _fwd1_kernel(m_ref, z_ref, a_ref, zs, outs):
    nz, two_h = z_ref.shape[1], m_ref.shape[1]
    _fill_chunks(zs, lambda ln: z_ref[0, :, :, ln].reshape(nz * S2B, LANE))
    for j in range(S2B):
        _set_rows_at(outs, j, S2B, _dot(m_ref[j], _rows_at(zs, j, nz, S2B).astype(BF16)))
    for c in range(HY_W // LANE):
        a_ref[0, :, :, c * LANE:(c + 1) * LANE] = outs[c].reshape(two_h, S2B, LANE).astype(BF16)


def _fwd1(z, m1):
    B, rows, _ = z.shape
    nz = rows // BLK
    two_h = m1.shape[1]
    blk = lambda r: pl.BlockSpec((1, r, S2B, HY_W), lambda b, s: (b, 0, s, 0))
    return pl.pallas_call(
        _fwd1_kernel,
        grid=(B, BLK // S2B),
        in_specs=[pl.BlockSpec((S2B, two_h, nz), lambda b, s: (s, 0, 0)), blk(nz)],
        out_specs=blk(two_h),
        out_shape=jax.ShapeDtypeStruct((B, two_h, BLK, HY_W), BF16),
        scratch_shapes=[_lane_chunk_scratch(nz * S2B), _lane_chunk_scratch(two_h * S2B)],
        compiler_params=_cp("parallel", "parallel"),
        name="hyena_fwd1",
    )(m1, z.reshape(B, nz, BLK, HY_W))


def _mid_kernel(n1h, ff_ref, fi_ref, ar_ref, ai_ref, kr_ref, ki_ref, gr_ref, gi_ref):
    k = pl.program_id(0)

    @pl.when(k * KB < n1h)
    def _():
        for t in range(KB):
            x = _dot(ff_ref[...], jnp.concatenate([ar_ref[0, t], ai_ref[0, t]], axis=0))
            xr, xi = x[:BLK], x[BLK:]
            kr, ki = kr_ref[t], ki_ref[t]
            y = jnp.concatenate([xr * kr - xi * ki, xr * ki + xi * kr], axis=0).astype(BF16)
            g = _dot(fi_ref[...], y)
            gr_ref[0, t] = g[:BLK].astype(BF16)
            gi_ref[0, t] = g[BLK:].astype(BF16)

    @pl.when(k * KB >= n1h)
    def _():
        gr_ref[...] = jnp.zeros_like(gr_ref)
        gi_ref[...] = jnp.zeros_like(gi_ref)


def _mid(a, kr, ki, ff, fi, n1h):
    B, two_h, _, _ = a.shape
    n1hp = two_h // 2
    assert n1hp % KB == 0
    blk = lambda off: pl.BlockSpec((1, KB, BLK, HY_W), lambda k, b: (b, k + off // KB, 0, 0))
    kblk = pl.BlockSpec((KB, BLK, HY_W), lambda k, b: (k, 0, 0))
    return pl.pallas_call(
        functools.partial(_mid_kernel, n1h),
        grid=(n1hp // KB, B),
        in_specs=[_const_spec(ff.shape), _const_spec(fi.shape), blk(0), blk(n1hp), kblk, kblk],
        out_specs=[blk(0), blk(0)],
        out_shape=[jax.ShapeDtypeStruct((B, n1hp, BLK, HY_W), BF16)] * 2,
        compiler_params=_cp("parallel", "parallel"),
        name="hyena_mid",
    )(ff, fi, a, a, kr, ki)


def _inv1_kernel(m_ref, gr_ref, gi_ref, y_ref, grs, gis, outs):
    n1hp, nz = gr_ref.shape[1], m_ref.shape[1]
    _fill_chunks(grs, lambda ln: gr_ref[0, :, :, ln].reshape(n1hp * S2B, LANE))
    _fill_chunks(gis, lambda ln: gi_ref[0, :, :, ln].reshape(n1hp * S2B, LANE))
    for j in range(S2B):
        g = jnp.concatenate([_rows_at(grs, j, n1hp, S2B), _rows_at(gis, j, n1hp, S2B)], axis=0)
        _set_rows_at(outs, j, S2B, _dot(m_ref[j], g.astype(BF16)))
    for c in range(HY_W // LANE):
        y_ref[0, :, :, c * LANE:(c + 1) * LANE] = outs[c].reshape(nz, S2B, LANE)


def _inv1(gr, gi, minv):
    B, n1hp, _, _ = gr.shape
    nz = minv.shape[1]
    blk = lambda r: pl.BlockSpec((1, r, S2B, HY_W), lambda b, s: (b, 0, s, 0))
    out = pl.pallas_call(
        _inv1_kernel,
        grid=(B, BLK // S2B),
        in_specs=[pl.BlockSpec((S2B, nz, 2 * n1hp), lambda b, s: (s, 0, 0)), blk(n1hp), blk(n1hp)],
        out_specs=blk(nz),
        out_shape=jax.ShapeDtypeStruct((B, nz, BLK, HY_W), F32),
        scratch_shapes=[_lane_chunk_scratch(n1hp * S2B), _lane_chunk_scratch(n1hp * S2B),
                        _lane_chunk_scratch(nz * S2B)],
        compiler_params=_cp("parallel", "parallel"),
        name="hyena_inv1",
    )(minv, gr, gi)
    return out.reshape(B, nz * BLK, HY_W)


def _filter_kernel(L, tm, w1_ref, b1_ref, w2_ref, b2_ref, w3_ref, fr_ref, dl_ref, k_ref, sum_ref):
    i = pl.program_id(0)
    m = i * tm + lax.broadcasted_iota(jnp.int32, (tm, 1), 0)
    mf = m.astype(F32)
    t = mf * (1.0 / (L - 1))
    wpos = mf * (2.0 * math.pi / L)
    bands = (HY_EMB - 1) // 2
    lane = lax.broadcasted_iota(jnp.int32, (1, LANE), 1)
    band = jnp.where(lane <= bands, lane - 1, lane - 1 - bands)
    f = 1e-4 + ((bands - 1 - 1e-4) / (bands - 1)) * band.astype(F32)
    ang = f * wpos
    feat = jnp.where(lane == 0, t,
                     jnp.where(lane <= bands, jnp.cos(ang),
                               jnp.where(lane < HY_EMB, -jnp.sin(ang), 0.0)))
    hp = lax.Precision.HIGHEST
    dot = lambda a, b: jnp.dot(a, b, precision=hp, preferred_element_type=F32)
    fr = fr_ref[...]
    h = jnp.sin(fr * (dot(feat, w1_ref[...]) + b1_ref[...]))
    h = jnp.sin(fr * (dot(h, w2_ref[...]) + b2_ref[...]))
    kf = dot(h, w3_ref[...])
    decay = jnp.exp(-t * dl_ref[...])
    decay2 = jnp.concatenate([decay, decay], axis=1)
    col = lax.broadcasted_iota(jnp.int32, (1, 2 * HY_W), 1)
    keep = (m < L) & ((col < HY_W) | (m >= 1))
    k = jnp.where(keep, kf * decay2, 0.0)
    k_ref[...] = k

    @pl.when(i == 0)
    def _():
        sum_ref[...] = jnp.zeros_like(sum_ref)

    sum_ref[...] += jnp.sum(jnp.abs(k).reshape(tm // 8, 8, 2 * HY_W), axis=0)


def _filter_taps(L, nz, w1, b1, w2, b2, w3, freq, deltas):
    rows = nz * BLK
    tm = _row_tile(rows, 1024, 8)
    args = (w1, b1, w2, b2, w3, freq, deltas)
    return pl.pallas_call(
        functools.partial(_filter_kernel, L, tm),
        grid=(rows // tm,),
        in_specs=[_const_spec(a.shape) for a in args],
        out_specs=[pl.BlockSpec((tm, 2 * HY_W), lambda i: (i, 0)),
                   pl.BlockSpec((8, 2 * HY_W), lambda i: (0, 0))],
        out_shape=[jax.ShapeDtypeStruct((rows, 2 * HY_W), F32),
                   jax.ShapeDtypeStruct((8, 2 * HY_W), F32)],
        compiler_params=_cp("arbitrary"),
        name="hyena_filter",
    )(*args)


def _kfwd1_kernel(n1hp, mh_ref, ml_ref, mmh_ref, mml_ref, mzh_ref, mzl_ref, kf_ref, kbm_ref, kbz_ref,
                  a_ref, kfs, kbms, kbzs, outs):
    nz = kf_ref.shape[0]
    for scr, ref in ((kfs, kf_ref), (kbms, kbm_ref), (kbzs, kbz_ref)):
        _fill_chunks(scr, lambda ln, ref=ref: ref[:, :, ln].reshape(nz * FS2B, LANE))
    for t in range(FS2B):
        fwd = _dot_hl(mh_ref[t], ml_ref[t], _rows_at(kfs, t, nz, FS2B))
        if t == 0:
            bwd = _dot_hl(mzh_ref[0], mzl_ref[0], _rows_at(kbzs, 0, nz, FS2B))
        else:
            bwd = _dot_hl(mmh_ref[FS2B - t], mml_ref[FS2B - t], _rows_at(kbms, FS2B - t, nz, FS2B))
        _set_rows_at(outs, t, FS2B, fwd + jnp.concatenate([bwd[:n1hp], -bwd[n1hp:]], axis=0))
    for c in range(HY_W // LANE):
        a_ref[:, :, c * LANE:(c + 1) * LANE] = outs[c].reshape(2 * n1hp, FS2B, LANE)


def _kfwd1(km, m1h, m1l):
    rows = km.shape[0]
    nz = rows // BLK
    two_h = m1h.shape[1]
    nblk = BLK // FS2B
    same = lambda i: i
    mirror = lambda i: nblk - 1 - i
    wrap = lambda i: (nblk - i) % nblk
    mat = lambda f: pl.BlockSpec((FS2B, two_h, nz), lambda i: (f(i), 0, 0))
    mat0 = pl.BlockSpec((1, two_h, nz), lambda i: (wrap(i) * FS2B, 0, 0))
    taps = lambda f, half: pl.BlockSpec((nz, FS2B, HY_W), lambda i: (0, f(i), half))
    return pl.pallas_call(
        functools.partial(_kfwd1_kernel, two_h // 2),
        grid=(nblk,),
        in_specs=[mat(same), mat(same), mat(mirror), mat(mirror), mat0, mat0,
                  taps(same, 0), taps(mirror, 1), taps(wrap, 1)],
        out_specs=pl.BlockSpec((two_h, FS2B, HY_W), lambda i: (0, i, 0)),
        out_shape=jax.ShapeDtypeStruct((two_h, BLK, HY_W), F32),
        scratch_shapes=[_lane_chunk_scratch(nz * FS2B)] * 3 + [_lane_chunk_scratch(two_h * FS2B)],
        compiler_params=_cp("parallel"),
        name="hyena_filter_fwd1",
    )(m1h, m1l, m1h, m1l, m1h, m1l, *([km.reshape(nz, BLK, 2 * HY_W)] * 3))


def _kmid_kernel(ffh_ref, ffl_ref, ar_ref, ai_ref, sum_ref, kr_ref, ki_ref):
    s = jnp.sum(sum_ref[...], axis=0, keepdims=True)
    inv = 1.0 / (s[:, :HY_W] + s[:, HY_W:])
    for t in range(KB):
        x = _dot_hl(ffh_ref[...], ffl_ref[...], jnp.concatenate([ar_ref[t], ai_ref[t]], axis=0))
        kr_ref[t] = x[:BLK] * inv
        ki_ref[t] = x[BLK:] * inv


def _kmid(a, ksum, ffh, ffl):
    two_h = a.shape[0]
    n1hp = two_h // 2
    blk = lambda off: pl.BlockSpec((KB, BLK, HY_W), lambda k: (k + off // KB, 0, 0))
    return pl.pallas_call(
        _kmid_kernel,
        grid=(n1hp // KB,),
        in_specs=[_const_spec(ffh.shape), _const_spec(ffl.shape), blk(0), blk(n1hp),
                  _const_spec(ksum.shape)],
        out_specs=[blk(0), blk(0)],
        out_shape=[jax.ShapeDtypeStruct((n1hp, BLK, HY_W), F32)] * 2,
        compiler_params=_cp("parallel"),
        name="hyena_filter_mid",
    )(ffh, ffl, a, a, ksum)


def _merge_kernel(x_ref, at_ref, y_ref, z_ref, x0_ref, po_ref, g_ref, sk_ref, wg_ref, wa_ref, wh_ref,
                  wp_ref, wo_ref, o_ref):
    x = x_ref[0]
    h = _rms(x, g_ref[...]).astype(BF16)
    hy = ((y_ref[0] + z_ref[0] * sk_ref[...]) * x0_ref[0]).astype(BF16)
    merged = None
    for n, (br, w_ref) in enumerate(((at_ref[0], wa_ref), (hy, wh_ref), (po_ref[0], wp_ref))):
        gate = jax.nn.sigmoid(_dot(h, wg_ref[:, n * D_MODEL:(n + 1) * D_MODEL]))
        term = gate * _dot(br, w_ref[...])
        merged = term if merged is None else merged + term
    o_ref[0] = x + _dot(merged.astype(BF16), wo_ref[...])


def _merge(x, attn, y, z, x0, po, g, skip, wg, wa, wh, wp, wo):
    B, Lp, _ = x.shape
    tm = _row_tile(Lp, 704)
    row = lambda w: pl.BlockSpec((1, tm, w), lambda b, i: (b, i, 0))
    weights = (g, skip, wg, wa, wh, wp, wo)
    return pl.pallas_call(
        _merge_kernel,
        grid=(B, Lp // tm),
        in_specs=[row(D_MODEL), row(ATTN_W), row(HY_W), row(HY_W), row(HY_W), row(POOL_W)]
                 + [_const_spec(w.shape) for w in weights],
        out_specs=row(D_MODEL),
        out_shape=jax.ShapeDtypeStruct((B, Lp, D_MODEL), F32),
        compiler_params=_cp("parallel", "parallel"),
        name="merge",
    )(x, attn, y, z, x0, po, *weights)


def _ffn_kernel(Lp, tm, fc, xp_ref, x_ref, xn_ref, g_ref, wg_ref, wu_ref, cw_ref, cb_ref, wd_ref, o_ref):
    i = pl.program_id(1)
    ext = _with_halo(xp_ref, x_ref, xn_ref, i * tm, tm, Lp)
    h = _rms(ext, g_ref[...]).astype(BF16)
    hm = h[HALO:HALO + tm]
    acc = x_ref[0]
    for c in range(D_FF // fc):
        sl = slice(c * fc, (c + 1) * fc)
        gp = _dot(h, wg_ref[:, sl])
        cw = cw_ref[:, sl]
        conv = (cw[0:1] * _shift(gp, 1) + cw[1:2] * gp + cw[2:3] * _shift(gp, -1))[HALO:HALO + tm]
        conv = conv + cb_ref[:, sl]
        act = 0.5 * conv * (1.0 + lax.erf(conv * (1.0 / math.sqrt(2.0))))
        up = _dot(hm, wu_ref[:, sl])
        acc = acc + _dot((act * up).astype(BF16), wd_ref[sl, :])
    o_ref[0] = acc


def _ffn(x, g, wg, wu, cw, cb, wd):
    B, Lp, _ = x.shape
    tm = _row_tile(Lp, 704)
    fc = 1408
    main, prev, nxt = _halo_specs(tm, Lp, D_MODEL)
    weights = (g, wg, wu, cw, cb, wd)
    return pl.pallas_call(
        functools.partial(_ffn_kernel, Lp, tm, fc),
        grid=(B, Lp // tm),
        in_specs=[prev, main, nxt] + [_const_spec(w.shape) for w in weights],
        out_specs=main,
        out_shape=jax.ShapeDtypeStruct((B, Lp, D_MODEL), F32),
        compiler_params=_cp("parallel", "parallel"),
        name="ffn",
    )(x, x, x, *weights)


def _t5_buckets(rel):
    half = N_BUCKETS // 2
    ret = (rel > 0).astype(np.int32) * half
    n = np.abs(rel)
    max_exact = half // 2
    large = max_exact + (np.log(np.maximum(n, 1) / max_exact) / np.log(MAX_DIST / max_exact)
                         * (half - max_exact)).astype(np.int32)
    large = np.minimum(large, half - 1)
    return (ret + np.where(n < max_exact, n, large)).astype(np.int32)


def _bias_table(rel_bias):
    rel = np.arange(3 * BLK)[None, :] - BLK - np.arange(BLK)[:, None]
    band = np.abs(rel) <= WINDOW
    onehot = jnp.asarray(_t5_buckets(rel).reshape(-1)[None, :] == np.arange(N_BUCKETS)[:, None])
    onehot = onehot.astype(F32)
    bias = jnp.dot(rel_bias.astype(F32).T, onehot, precision=lax.Precision.HIGHEST)
    bias = jnp.where(band.reshape(1, -1), bias, NEG)
    return bias.reshape(N_HEADS * BLK, 3 * BLK)


def _dft_tables(n1, n1hp, nz):
    n = n1 * BLK
    n1h = n1 // 2 + 1

    k1 = np.arange(n1hp)
    a1 = 2.0 * np.pi * ((k1[:, None] * np.arange(nz)[None, :]) % n1) / n1
    a2 = 2.0 * np.pi * ((np.arange(BLK)[:, None] * k1[None, :]) % n) / n
    c1, s1, c2, s2 = (jnp.asarray(t, F32) for t in (np.cos(a1), np.sin(a1), np.cos(a2), np.sin(a2)))
    live = jnp.asarray(k1 < n1h, F32)
    w = jnp.asarray(np.where(k1 >= n1h, 0.0, np.where((k1 == 0) | (k1 == n1 // 2), 1.0, 2.0)) / n, F32)

    def cis(c2b, s2b, c1b, s1b, scale):
        return (c1b * c2b - s1b * s2b) * scale, -(s1b * c2b + c1b * s2b) * scale

    re, im = cis(c2[:, :, None], s2[:, :, None], c1[None], s1[None], live[None, :, None])
    m1 = jnp.concatenate([re, im], axis=1)
    re, im = cis(c2[:, None, :], s2[:, None, :], c1.T[None], s1.T[None], w[None, None, :])
    minv = jnp.concatenate([re, im], axis=2).astype(BF16)
    k2 = np.arange(BLK)
    a2 = 2.0 * np.pi * ((k2[:, None] * k2[None, :]) % BLK) / BLK
    fr, fi = np.cos(a2), -np.sin(a2)
    ff = _split(jnp.asarray(np.block([[fr, -fi], [fi, fr]]), F32))
    fv = jnp.asarray(np.block([[fr, fi], [-fi, fr]]), BF16)
    return _split(m1), minv, ff, fv


def _dup_heads(w):
    d = w.shape[0]
    w = w.reshape(d, N_KV, 1, HEAD_DIM)
    return jnp.broadcast_to(w, (d, N_KV, 2, HEAD_DIM)).reshape(d, 2 * KV_W)


def _geometry(L):
    Lp = _round_up(L + HALO, BLK)
    nz = _round_up(Lp // BLK, 16)
    n1 = _round_up(pl.cdiv(2 * L - 1, BLK), 2)
    n1h = n1 // 2 + 1
    n1hp = _round_up(n1h, 16)
    return Lp, nz, n1, n1h, n1hp


def _layer(x, L, geom, tabs, bias, p):
    Lp, nz, n1, n1h, n1hp = geom
    (m1h, m1l), minv, (ffh, ffl), fv = tabs
    q, k, v, u_hy, u_po = _inproj(x, p["norm_mix"], p["wq"], p["wk"], p["wv"], p["wh"], p["wp"],
                                  p["q_gain"], p["k_gain"], p["e"])
    attn = _attention(q, k, v, bias, p["sink"], L)
    po = _pool(u_po, p["pool_w"], p["pool_scale"], L)

    km, ksum = _filter_taps(L, nz, p["hy_w1"], p["hy_b1"], p["hy_w2"], p["hy_b2"], p["hy_w3"],
                            p["hy_freq"], p["deltas"])
    kr, ki = _kmid(_kfwd1(km, m1h, m1l), ksum, ffh, ffl)
    z, x0 = _hy_prep(u_hy, p["hy_conv_w"], p["hy_conv_b"], L, nz)
    gr, gi = _mid(_fwd1(z, m1h), kr, ki, ffh, fv, n1h)
    y = _inv1(gr, gi, minv)

    x = _merge(x, attn, y, z, x0, po, p["norm_mix"], p["hy_skip"], p["wgates"], p["w_attn_o"],
               p["w_hyena_o"], p["w_pool_o"], p["w_out"])
    return _ffn(x, p["norm_ffn"], p["w_gate"], p["w_up"], p["ffn_conv_w"], p["ffn_conv_b"],
                p["w_down"])


def _trunk(x, meta, bias, layers):
    B, S, _ = x.shape
    L = S + N_META
    geom = _geometry(L)
    Lp, nz, n1, n1h, n1hp = geom
    tabs = _dft_tables(n1, n1hp, nz)
    h = jnp.concatenate([jnp.broadcast_to(meta[None], (B, N_META, D_MODEL)), x,
                         jnp.zeros((B, Lp - L, D_MODEL), F32)], axis=1)
    for p in layers:
        h = _layer(h, L, geom, tabs, bias, p)
    return h[:, N_META:L]


def _prep_layers(norm_mix, w_in, q_gain, k_gain, attn_sink, hy_conv_w, hy_conv_b, hy_w1, hy_b1,
                 hy_w2, hy_b2, hy_w3, hy_freq, hy_skip, pool_w, pool_scale, w_attn_o, w_hyena_o,
                 w_pool_o, w_out, norm_ffn, w_gate, w_up, ffn_conv_w, ffn_conv_b, w_down):
    o1 = ATTN_W
    o2 = o1 + KV_W
    o3 = o2 + KV_W
    o4 = o3 + 3 * HY_W
    o5 = o4 + POOL_W
    lane = np.arange(LANE)
    e = jnp.asarray((lane[:, None] // HEAD_DIM) == (lane[None, :] // HEAD_DIM), BF16)
    deltas = jnp.abs(jnp.linspace(math.log(HY_DECAY_TARGET) / HY_FAST,
                                  math.log(HY_DECAY_TARGET) / HY_SLOW, HY_W, dtype=F32))[None]
    row = lambda a: a.astype(F32)[None]
    pad = lambda a, r, c: jnp.pad(a.astype(F32), ((0, r - a.shape[0]), (0, c - a.shape[1])))
    layers = []
    for l in range(norm_mix.shape[0]):
        w = w_in[l]
        layers.append(dict(
            norm_mix=row(norm_mix[l]),
            wq=w[:, :o1].astype(BF16),
            wk=_dup_heads(w[:, o1:o2]).astype(BF16),
            wv=_dup_heads(w[:, o2:o3]).astype(BF16),
            wh=w[:, o3:o4].astype(BF16),
            wp=w[:, o4:o5].astype(BF16),
            wgates=w[:, o5:].astype(BF16),
            q_gain=jnp.tile(row(q_gain[l]), (1, N_HEADS)) * (HEAD_DIM ** -0.5),
            k_gain=jnp.tile(row(k_gain[l]), (1, 2 * N_KV)),
            e=e,
            sink=attn_sink[l].astype(F32),
            hy_conv_w=hy_conv_w[l].astype(F32), hy_conv_b=row(hy_conv_b[l]),
            hy_w1=pad(hy_w1[l], LANE, LANE), hy_b1=pad(row(hy_b1[l]), 1, LANE),
            hy_w2=pad(hy_w2[l], LANE, LANE), hy_b2=pad(row(hy_b2[l]), 1, LANE),
            hy_w3=pad(hy_w3[l], LANE, 2 * HY_W), hy_freq=pad(row(hy_freq[l]), 1, LANE),
            hy_skip=row(hy_skip[l]), deltas=deltas,
            pool_w=pool_w[l].astype(BF16), pool_scale=row(pool_scale[l]),
            w_attn_o=w_attn_o[l].astype(BF16), w_hyena_o=w_hyena_o[l].astype(BF16),
            w_pool_o=w_pool_o[l].astype(BF16), w_out=w_out[l].astype(BF16),
            norm_ffn=row(norm_ffn[l]), w_gate=w_gate[l].astype(BF16), w_up=w_up[l].astype(BF16),
            ffn_conv_w=ffn_conv_w[l].astype(F32), ffn_conv_b=row(ffn_conv_b[l]),
            w_down=w_down[l].astype(BF16),
        ))
    return layers


def kernel(x_prompt, x_sample, meta_tokens, rel_bias, norm_mix, w_in, q_gain, k_gain, attn_sink,
           hy_conv_w, hy_conv_b, hy_w1, hy_b1, hy_w2, hy_b2, hy_w3, hy_freq, hy_skip, pool_w,
           pool_scale, w_attn_o, w_hyena_o, w_pool_o, w_out, norm_ffn, w_gate, w_up,
           ffn_conv_w, ffn_conv_b, w_down):
    layers = _prep_layers(norm_mix, w_in, q_gain, k_gain, attn_sink, hy_conv_w, hy_conv_b, hy_w1,
                          hy_b1, hy_w2, hy_b2, hy_w3, hy_freq, hy_skip, pool_w, pool_scale,
                          w_attn_o, w_hyena_o, w_pool_o, w_out, norm_ffn, w_gate, w_up,
                          ffn_conv_w, ffn_conv_b, w_down)
    bias = _bias_table(rel_bias)
    meta = meta_tokens.astype(F32)
    return (_trunk(x_prompt, meta, bias, layers), _trunk(x_sample, meta, bias, layers))
```

```python
import functools
import math

import numpy as np
import jax
import jax.numpy as jnp
from jax import lax
from jax.experimental import pallas as pl
from jax.experimental.pallas import tpu as pltpu

F32 = jnp.float32
BF16 = jnp.bfloat16

D_MODEL = 1024
N_META = 16
N_HEADS = 8
N_KV = 2
HEAD_DIM = 64
ATTN_W = N_HEADS * HEAD_DIM
KV_W = N_KV * HEAD_DIM
WINDOW = 128
BLK = 128
N_BUCKETS = 32
MAX_DIST = 128
HY_W = 512
HY_EMB = 33
HY_HID = 64
HY_DECAY_TARGET = 1e-2
HY_FAST = 0.3
HY_SLOW = 1.5
POOL_W = 512
POOL_WINDOWS = (2, 4, 8, 16)
POOL_G = 128
D_FF = 2816
EPS = 1e-6
NEG = -1e30
HALO = 8
LANE = 128
VMEM_LIMIT = 56 << 20


def _cp(*sem):
    return pltpu.CompilerParams(dimension_semantics=sem, vmem_limit_bytes=VMEM_LIMIT)


def _round_up(a, m):
    return -(-a // m) * m


def _row_tile(rows, cap, mult=16):
    best = None
    for t in range(mult, min(rows, cap) + 1, mult):
        if rows % t == 0:
            best = t
    assert best is not None, (rows, cap)
    return best


def _const_spec(shape):
    nd = len(shape)
    return pl.BlockSpec(shape, lambda *_: (0,) * nd, pipeline_mode=pl.Buffered(1))


def _dot(a, b):
    return jnp.dot(a, b, preferred_element_type=F32)


def _split(a):
    hi = a.astype(BF16)
    lo = (a - hi.astype(F32)).astype(BF16)
    return hi, lo


def _rms(x, g):
    ms = jnp.mean(x * x, axis=-1, keepdims=True)
    return x * lax.rsqrt(ms + EPS) * g


def _head_meansq(a, e):
    hi, lo = _split(a * a)
    cols = []
    for j in range(a.shape[1] // LANE):
        sl = slice(j * LANE, (j + 1) * LANE)
        cols.append(_dot(hi[:, sl], e) + _dot(lo[:, sl], e))
    return jnp.concatenate(cols, axis=1) * (1.0 / HEAD_DIM)


def _inproj_kernel(x_ref, g_ref, wq_ref, wk_ref, wv_ref, wh_ref, wp_ref, qg_ref, kg_ref, e_ref,
                   q_ref, k_ref, v_ref, hy_ref, po_ref):
    h = _rms(x_ref[0], g_ref[...]).astype(BF16)
    e = e_ref[...]
    q = _dot(h, wq_ref[...])
    q_ref[0] = (q * lax.rsqrt(_head_meansq(q, e) + EPS) * qg_ref[...]).astype(BF16)
    k = _dot(h, wk_ref[...])
    k_ref[0] = (k * lax.rsqrt(_head_meansq(k, e) + EPS) * kg_ref[...]).astype(BF16)
    v_ref[0] = _dot(h, wv_ref[...]).astype(BF16)
    hy_ref[0] = _dot(h, wh_ref[...])
    po_ref[0] = _dot(h, wp_ref[...])


def _inproj(x, g, wq, wk, wv, wh, wp, qg, kg, e):
    B, Lp, _ = x.shape
    tm = _row_tile(Lp, 704)
    row = lambda w: pl.BlockSpec((1, tm, w), lambda b, i: (b, i, 0))
    return pl.pallas_call(
        _inproj_kernel,
        grid=(B, Lp // tm),
        in_specs=[row(D_MODEL), _const_spec(g.shape), _const_spec(wq.shape), _const_spec(wk.shape),
                  _const_spec(wv.shape), _const_spec(wh.shape), _const_spec(wp.shape),
                  _const_spec(qg.shape), _const_spec(kg.shape), _const_spec(e.shape)],
        out_specs=[row(ATTN_W), row(2 * KV_W), row(2 * KV_W), row(3 * HY_W), row(POOL_W)],
        out_shape=[jax.ShapeDtypeStruct((B, Lp, ATTN_W), BF16),
                   jax.ShapeDtypeStruct((B, Lp, 2 * KV_W), BF16),
                   jax.ShapeDtypeStruct((B, Lp, 2 * KV_W), BF16),
                   jax.ShapeDtypeStruct((B, Lp, 3 * HY_W), F32),
                   jax.ShapeDtypeStruct((B, Lp, POOL_W), F32)],
        compiler_params=_cp("parallel", "parallel"),
        name="inproj",
    )(x, g, wq, wk, wv, wh, wp, qg, kg, e)


ATTN_ROWS = 32


def _attn_kernel(L, sink_ref, q_ref, kp_ref, ko_ref, kn_ref, vp_ref, vo_ref, vn_ref, bias_ref, o_ref,
                 s_scr, p_scr, r_scr):
    j = pl.program_id(1)
    q = q_ref[0]
    kwin = jnp.concatenate([kp_ref[0], ko_ref[0], kn_ref[0]], axis=0)
    vwin = jnp.concatenate([vp_ref[0], vo_ref[0], vn_ref[0]], axis=0)
    kpos = (j - 1) * BLK + lax.broadcasted_iota(jnp.int32, (1, 3 * BLK), 1)
    kvalid = (kpos >= 0) & (kpos < L)
    lane = lax.broadcasted_iota(jnp.int32, (BLK, LANE), 1)
    low = lane < HEAD_DIM
    zero = jnp.zeros((BLK, LANE), BF16)
    group = N_HEADS // N_KV
    outs = []
    for kv in range(N_KV):
        kk = kwin[:, kv * LANE:(kv + 1) * LANE]
        vv = vwin[:, kv * LANE:(kv + 1) * LANE]
        parts = []
        for c in range(2):
            qc = q[:, (2 * kv + c) * LANE:(2 * kv + c + 1) * LANE]
            parts += [jnp.where(low, qc, zero), jnp.where(low, zero, qc)]
        qm = jnp.concatenate(parts, axis=0)
        s_scr[...] = lax.dot_general(qm, kk, (((1,), (1,)), ((), ())), preferred_element_type=F32)
        for r0 in range(0, group * BLK, ATTN_ROWS):
            rows = slice(r0, r0 + ATTN_ROWS)
            s = s_scr[rows, :] + bias_ref[kv * group * BLK + r0:kv * group * BLK + r0 + ATTN_ROWS, :]
            s = jnp.where(kvalid, s, NEG)
            sink = sink_ref[kv * group + r0 // BLK]
            m = jnp.maximum(jnp.max(s, axis=-1, keepdims=True), sink)
            p = jnp.exp(s - m)
            denom = jnp.sum(p, axis=-1, keepdims=True) + jnp.exp(sink - m)
            p_scr[rows, :] = p.astype(BF16)
            r_scr[rows, :] = jnp.broadcast_to(1.0 / denom, (ATTN_ROWS, LANE))
        o = _dot(p_scr[...], vv) * r_scr[...]
        for c in range(2):
            outs.append(jnp.where(low, o[(2 * c) * BLK:(2 * c + 1) * BLK],
                                  o[(2 * c + 1) * BLK:(2 * c + 2) * BLK]))
    out = jnp.concatenate(outs, axis=1)
    qpos = j * BLK + lax.broadcasted_iota(jnp.int32, (BLK, 1), 0)
    o_ref[0] = jnp.where(qpos < L, out, 0.0).astype(BF16)


def _attention(q, k, v, bias, sink, L):
    B, Lp, _ = q.shape
    nb = Lp // BLK
    cur = lambda w: pl.BlockSpec((1, BLK, w), lambda b, j: (b, j, 0))
    prv = lambda w: pl.BlockSpec((1, BLK, w), lambda b, j: (b, jnp.maximum(j - 1, 0), 0))
    nxt = lambda w: pl.BlockSpec((1, BLK, w), lambda b, j: (b, jnp.minimum(j + 1, nb - 1), 0))
    kw = 2 * KV_W
    return pl.pallas_call(
        functools.partial(_attn_kernel, L),
        grid=(B, nb),
        in_specs=[pl.BlockSpec(memory_space=pltpu.SMEM), cur(ATTN_W), prv(kw), cur(kw), nxt(kw),
                  prv(kw), cur(kw), nxt(kw), _const_spec(bias.shape)],
        out_specs=cur(ATTN_W),
        out_shape=jax.ShapeDtypeStruct((B, Lp, ATTN_W), BF16),
        scratch_shapes=[pltpu.VMEM((N_HEADS // N_KV * BLK, 3 * BLK), F32),
                        pltpu.VMEM((N_HEADS // N_KV * BLK, 3 * BLK), BF16),
                        pltpu.VMEM((N_HEADS // N_KV * BLK, LANE), F32)],
        compiler_params=_cp("parallel", "parallel"),
        name="attention",
    )(sink, q, k, k, k, v, v, v, bias)


def _halo_specs(tm, rows, width):
    r8 = tm // HALO
    last = rows // HALO - 1
    main = pl.BlockSpec((1, tm, width), lambda b, i: (b, i, 0))
    prev = pl.BlockSpec((1, HALO, width), lambda b, i: (b, jnp.maximum(i * r8 - 1, 0), 0))
    nxt = pl.BlockSpec((1, HALO, width), lambda b, i: (b, jnp.minimum((i + 1) * r8, last), 0))
    return main, prev, nxt


def _with_halo(prev_ref, main_ref, next_ref, tile_start, tm, rows):
    pos = tile_start - HALO + lax.broadcasted_iota(jnp.int32, (tm + 2 * HALO, 1), 0)
    ext = jnp.concatenate([prev_ref[0], main_ref[0], next_ref[0]], axis=0)
    return jnp.where((pos >= 0) & (pos < rows), ext, 0.0)


def _shift(a, s):
    return pltpu.roll(a, s % a.shape[0], axis=0)


def _pool_kernel(L, Lp, tm, up_ref, u_ref, un_ref, w_ref, sc_ref, o_ref):
    i = pl.program_id(1)
    ext = _with_halo(up_ref, u_ref, un_ref, i * tm, tm, Lp)
    pos = i * tm + lax.broadcasted_iota(jnp.int32, (tm, 1), 0)
    outs = []
    for g, win in enumerate(POOL_WINDOWS):
        a = ext[:, g * POOL_G:(g + 1) * POOL_G]
        s = a + _shift(a, 1)
        half = 1
        while 2 * half < win:
            s = _shift(s, half) + _shift(s, -half)
            half *= 2
        lo = jnp.maximum(pos - win // 2, 0)
        hi = jnp.minimum(pos - win // 2 + win, L)
        cnt = jnp.maximum(hi - lo, 1).astype(F32)
        pooled = s[HALO:HALO + tm] / cnt - a[HALO:HALO + tm]
        pooled = jnp.where(pos < L, pooled, 0.0)
        outs.append(_dot(pooled.astype(BF16), w_ref[g]))
    o_ref[0] = (jnp.concatenate(outs, axis=1) * sc_ref[...]).astype(BF16)


def _pool(u, w, scale, L):
    B, Lp, _ = u.shape
    tm = _row_tile(Lp, 1408)
    main, prev, nxt = _halo_specs(tm, Lp, POOL_W)
    return pl.pallas_call(
        functools.partial(_pool_kernel, L, Lp, tm),
        grid=(B, Lp // tm),
        in_specs=[prev, main, nxt, _const_spec(w.shape), _const_spec(scale.shape)],
        out_specs=main,
        out_shape=jax.ShapeDtypeStruct((B, Lp, POOL_W), BF16),
        compiler_params=_cp("parallel", "parallel"),
        name="pool_mixer",
    )(u, u, u, w, scale)


def _hy_prep_kernel(L, Lp, tm, up_ref, u_ref, un_ref, cw_ref, cb_ref, z_ref, x0_ref):
    i = pl.program_id(1)
    ext = _with_halo(up_ref, u_ref, un_ref, i * tm, tm, Lp)
    cw = cw_ref[...]
    u = (cw[0:1] * _shift(ext, 1) + cw[1:2] * ext + cw[2:3] * _shift(ext, -1))[HALO:HALO + tm]
    u = u + cb_ref[...]
    pos = i * tm + lax.broadcasted_iota(jnp.int32, (tm, 1), 0)
    valid = pos < L
    z_ref[0] = jnp.where(valid, u[:, 2 * HY_W:] * u[:, HY_W:2 * HY_W], 0.0)
    x0_ref[0] = jnp.where(valid, u[:, :HY_W], 0.0)


def _hy_prep(u, cw, cb, L, nz):
    B, Lp, _ = u.shape
    tm = 1024
    rows = nz * BLK
    assert rows % tm == 0
    last_blk = pl.cdiv(Lp, tm) - 1
    last8 = Lp // HALO - 1
    r8 = tm // HALO
    main = pl.BlockSpec((1, tm, 3 * HY_W), lambda b, i: (b, jnp.minimum(i, last_blk), 0))
    prev = pl.BlockSpec((1, HALO, 3 * HY_W),
                        lambda b, i: (b, jnp.clip(i * r8 - 1, 0, last8), 0))
    nxt = pl.BlockSpec((1, HALO, 3 * HY_W),
                       lambda b, i: (b, jnp.minimum((i + 1) * r8, last8), 0))
    out = pl.BlockSpec((1, tm, HY_W), lambda b, i: (b, i, 0))
    return pl.pallas_call(
        functools.partial(_hy_prep_kernel, L, Lp, tm),
        grid=(B, rows // tm),
        in_specs=[prev, main, nxt, _const_spec(cw.shape), _const_spec(cb.shape)],
        out_specs=[out, out],
        out_shape=[jax.ShapeDtypeStruct((B, rows, HY_W), F32)] * 2,
        compiler_params=_cp("parallel", "parallel"),
        name="hyena_prep",
    )(u, u, u, cw, cb)


def _dot_hl(m_hi, m_lo, d):
    d_hi, d_lo = _split(d)
    return _dot(m_hi, d_hi) + _dot(m_hi, d_lo) + _dot(m_lo, d_hi)


S2B = 16
FS2B = 8
KB = 2


def _lane_chunk_scratch(rows):
    return pltpu.VMEM((HY_W // LANE, rows, LANE), F32)


def _fill_chunks(scr, load_chunk):
    for c in range(HY_W // LANE):
        scr[c] = load_chunk(slice(c * LANE, (c + 1) * LANE)).astype(F32)


def _rows_at(scr, j, count, stride):
    return jnp.concatenate([scr[c, pl.ds(j, count, stride=stride), :]
                            for c in range(HY_W // LANE)], axis=1)


def _set_rows_at(scr, j, stride, val):
    for c in range(HY_W // LANE):
        scr[c, pl.ds(j, val.shape[0], stride=stride), :] = val[:, c * LANE:(c + 1) * LANE]


def _fwd1_kernel(m_ref, z_ref, a_ref, zs, outs):
    nz, two_h = z_ref.shape[1], m_ref.shape[1]
    _fill_chunks(zs, lambda ln: z_ref[0, :, :, ln].reshape(nz * S2B, LANE))
    for j in range(S2B):
        _set_rows_at(outs, j, S2B, _dot(m_ref[j], _rows_at(zs, j, nz, S2B).astype(BF16)))
    for c in range(HY_W // LANE):
        a_ref[0, :, :, c * LANE:(c + 1) * LANE] = outs[c].reshape(two_h, S2B, LANE).astype(BF16)


def _fwd1(z, m1):
    B, rows, _ = z.shape
    nz = rows // BLK
    two_h = m1.shape[1]
    blk = lambda r: pl.BlockSpec((1, r, S2B, HY_W), lambda b, s: (b, 0, s, 0))
    return pl.pallas_call(
        _fwd1_kernel,
        grid=(B, BLK // S2B),
        in_specs=[pl.BlockSpec((S2B, two_h, nz), lambda b, s: (s, 0, 0)), blk(nz)],
        out_specs=blk(two_h),
        out_shape=jax.ShapeDtypeStruct((B, two_h, BLK, HY_W), BF16),
        scratch_shapes=[_lane_chunk_scratch(nz * S2B), _lane_chunk_scratch(two_h * S2B)],
        compiler_params=_cp("parallel", "parallel"),
        name="hyena_fwd1",
    )(m1, z.reshape(B, nz, BLK, HY_W))


def _mid_kernel(n1h, ff_ref, fi_ref, ar_ref, ai_ref, kr_ref, ki_ref, gr_ref, gi_ref):
    k = pl.program_id(0)

    @pl.when(k * KB < n1h)
    def _():
        for t in range(KB):
            x = _dot(ff_ref[...], jnp.concatenate([ar_ref[0, t], ai_ref[0, t]], axis=0))
            xr, xi = x[:BLK], x[BLK:]
            kr, ki = kr_ref[t], ki_ref[t]
            y = jnp.concatenate([xr * kr - xi * ki, xr * ki + xi * kr], axis=0).astype(BF16)
            g = _dot(fi_ref[...], y)
            gr_ref[0, t] = g[:BLK].astype(BF16)
            gi_ref[0, t] = g[BLK:].astype(BF16)

    @pl.when(k * KB >= n1h)
    def _():
        gr_ref[...] = jnp.zeros_like(gr_ref)
        gi_ref[...] = jnp.zeros_like(gi_ref)


def _mid(a, kr, ki, ff, fi, n1h):
    B, two_h, _, _ = a.shape
    n1hp = two_h // 2
    assert n1hp % KB == 0
    blk = lambda off: pl.BlockSpec((1, KB, BLK, HY_W), lambda k, b: (b, k + off // KB, 0, 0))
    kblk = pl.BlockSpec((KB, BLK, HY_W), lambda k, b: (k, 0, 0))
    return pl.pallas_call(
        functools.partial(_mid_kernel, n1h),
        grid=(n1hp // KB, B),
        in_specs=[_const_spec(ff.shape), _const_spec(fi.shape), blk(0), blk(n1hp), kblk, kblk],
        out_specs=[blk(0), blk(0)],
        out_shape=[jax.ShapeDtypeStruct((B, n1hp, BLK, HY_W), BF16)] * 2,
        compiler_params=_cp("parallel", "parallel"),
        name="hyena_mid",
    )(ff, fi, a, a, kr, ki)


def _inv1_kernel(m_ref, gr_ref, gi_ref, y_ref, grs, gis, outs):
    n1hp, nz = gr_ref.shape[1], m_ref.shape[1]
    _fill_chunks(grs, lambda ln: gr_ref[0, :, :, ln].reshape(n1hp * S2B, LANE))
    _fill_chunks(gis, lambda ln: gi_ref[0, :, :, ln].reshape(n1hp * S2B, LANE))
    for j in range(S2B):
        g = jnp.concatenate([_rows_at(grs, j, n1hp, S2B), _rows_at(gis, j, n1hp, S2B)], axis=0)
        _set_rows_at(outs, j, S2B, _dot(m_ref[j], g.astype(BF16)))
    for c in range(HY_W // LANE):
        y_ref[0, :, :, c * LANE:(c + 1) * LANE] = outs[c].reshape(nz, S2B, LANE)


def _inv1(gr, gi, minv):
    B, n1hp, _, _ = gr.shape
    nz = minv.shape[1]
    blk = lambda r: pl.BlockSpec((1, r, S2B, HY_W), lambda b, s: (b, 0, s, 0))
    out = pl.pallas_call(
        _inv1_kernel,
        grid=(B, BLK // S2B),
        in_specs=[pl.BlockSpec((S2B, nz, 2 * n1hp), lambda b, s: (s, 0, 0)), blk(n1hp), blk(n1hp)],
        out_specs=blk(nz),
        out_shape=jax.ShapeDtypeStruct((B, nz, BLK, HY_W), F32),
        scratch_shapes=[_lane_chunk_scratch(n1hp * S2B), _lane_chunk_scratch(n1hp * S2B),
                        _lane_chunk_scratch(nz * S2B)],
        compiler_params=_cp("parallel", "parallel"),
        name="hyena_inv1",
    )(minv, gr, gi)
    return out.reshape(B, nz * BLK, HY_W)


def _filter_kernel(L, tm, w1_ref, b1_ref, w2_ref, b2_ref, w3_ref, fr_ref, dl_ref, k_ref, sum_ref):
    i = pl.program_id(0)
    m = i * tm + lax.broadcasted_iota(jnp.int32, (tm, 1), 0)
    mf = m.astype(F32)
    t = mf * (1.0 / (L - 1))
    wpos = mf * (2.0 * math.pi / L)
    bands = (HY_EMB - 1) // 2
    lane = lax.broadcasted_iota(jnp.int32, (1, LANE), 1)
    band = jnp.where(lane <= bands, lane - 1, lane - 1 - bands)
    f = 1e-4 + ((bands - 1 - 1e-4) / (bands - 1)) * band.astype(F32)
    ang = f * wpos
    feat = jnp.where(lane == 0, t,
                     jnp.where(lane <= bands, jnp.cos(ang),
                               jnp.where(lane < HY_EMB, -jnp.sin(ang), 0.0)))
    hp = lax.Precision.HIGHEST
    dot = lambda a, b: jnp.dot(a, b, precision=hp, preferred_element_type=F32)
    fr = fr_ref[...]
    h = jnp.sin(fr * (dot(feat, w1_ref[...]) + b1_ref[...]))
    h = jnp.sin(fr * (dot(h, w2_ref[...]) + b2_ref[...]))
    kf = dot(h, w3_ref[...])
    decay = jnp.exp(-t * dl_ref[...])
    decay2 = jnp.concatenate([decay, decay], axis=1)
    col = lax.broadcasted_iota(jnp.int32, (1, 2 * HY_W), 1)
    keep = (m < L) & ((col < HY_W) | (m >= 1))
    k = jnp.where(keep, kf * decay2, 0.0)
    k_ref[...] = k

    @pl.when(i == 0)
    def _():
        sum_ref[...] = jnp.zeros_like(sum_ref)

    sum_ref[...] += jnp.sum(jnp.abs(k).reshape(tm // 8, 8, 2 * HY_W), axis=0)


def _filter_taps(L, nz, w1, b1, w2, b2, w3, freq, deltas):
    rows = nz * BLK
    tm = _row_tile(rows, 1024, 8)
    args = (w1, b1, w2, b2, w3, freq, deltas)
    return pl.pallas_call(
        functools.partial(_filter_kernel, L, tm),
        grid=(rows // tm,),
        in_specs=[_const_spec(a.shape) for a in args],
        out_specs=[pl.BlockSpec((tm, 2 * HY_W), lambda i: (i, 0)),
                   pl.BlockSpec((8, 2 * HY_W), lambda i: (0, 0))],
        out_shape=[jax.ShapeDtypeStruct((rows, 2 * HY_W), F32),
                   jax.ShapeDtypeStruct((8, 2 * HY_W), F32)],
        compiler_params=_cp("arbitrary"),
        name="hyena_filter",
    )(*args)


def _kfwd1_kernel(n1hp, mh_ref, ml_ref, mmh_ref, mml_ref, mzh_ref, mzl_ref, kf_ref, kbm_ref, kbz_ref,
                  a_ref, kfs, kbms, kbzs, outs):
    nz = kf_ref.shape[0]
    for scr, ref in ((kfs, kf_ref), (kbms, kbm_ref), (kbzs, kbz_ref)):
        _fill_chunks(scr, lambda ln, ref=ref: ref[:, :, ln].reshape(nz * FS2B, LANE))
    for t in range(FS2B):
        fwd = _dot_hl(mh_ref[t], ml_ref[t], _rows_at(kfs, t, nz, FS2B))
        if t == 0:
            bwd = _dot_hl(mzh_ref[0], mzl_ref[0], _rows_at(kbzs, 0, nz, FS2B))
        else:
            bwd = _dot_hl(mmh_ref[FS2B - t], mml_ref[FS2B - t], _rows_at(kbms, FS2B - t, nz, FS2B))
        _set_rows_at(outs, t, FS2B, fwd + jnp.concatenate([bwd[:n1hp], -bwd[n1hp:]], axis=0))
    for c in range(HY_W // LANE):
        a_ref[:, :, c * LANE:(c + 1) * LANE] = outs[c].reshape(2 * n1hp, FS2B, LANE)


def _kfwd1(km, m1h, m1l):
    rows = km.shape[0]
    nz = rows // BLK
    two_h = m1h.shape[1]
    nblk = BLK // FS2B
    same = lambda i: i
    mirror = lambda i: nblk - 1 - i
    wrap = lambda i: (nblk - i) % nblk
    mat = lambda f: pl.BlockSpec((FS2B, two_h, nz), lambda i: (f(i), 0, 0))
    mat0 = pl.BlockSpec((1, two_h, nz), lambda i: (wrap(i) * FS2B, 0, 0))
    taps = lambda f, half: pl.BlockSpec((nz, FS2B, HY_W), lambda i: (0, f(i), half))
    return pl.pallas_call(
        functools.partial(_kfwd1_kernel, two_h // 2),
        grid=(nblk,),
        in_specs=[mat(same), mat(same), mat(mirror), mat(mirror), mat0, mat0,
                  taps(same, 0), taps(mirror, 1), taps(wrap, 1)],
        out_specs=pl.BlockSpec((two_h, FS2B, HY_W), lambda i: (0, i, 0)),
        out_shape=jax.ShapeDtypeStruct((two_h, BLK, HY_W), F32),
        scratch_shapes=[_lane_chunk_scratch(nz * FS2B)] * 3 + [_lane_chunk_scratch(two_h * FS2B)],
        compiler_params=_cp("parallel"),
        name="hyena_filter_fwd1",
    )(m1h, m1l, m1h, m1l, m1h, m1l, *([km.reshape(nz, BLK, 2 * HY_W)] * 3))


def _kmid_kernel(ffh_ref, ffl_ref, ar_ref, ai_ref, sum_ref, kr_ref, ki_ref):
    s = jnp.sum(sum_ref[...], axis=0, keepdims=True)
    inv = 1.0 / (s[:, :HY_W] + s[:, HY_W:])
    for t in range(KB):
        x = _dot_hl(ffh_ref[...], ffl_ref[...], jnp.concatenate([ar_ref[t], ai_ref[t]], axis=0))
        kr_ref[t] = x[:BLK] * inv
        ki_ref[t] = x[BLK:] * inv


def _kmid(a, ksum, ffh, ffl):
    two_h = a.shape[0]
    n1hp = two_h // 2
    blk = lambda off: pl.BlockSpec((KB, BLK, HY_W), lambda k: (k + off // KB, 0, 0))
    return pl.pallas_call(
        _kmid_kernel,
        grid=(n1hp // KB,),
        in_specs=[_const_spec(ffh.shape), _const_spec(ffl.shape), blk(0), blk(n1hp),
                  _const_spec(ksum.shape)],
        out_specs=[blk(0), blk(0)],
        out_shape=[jax.ShapeDtypeStruct((n1hp, BLK, HY_W), F32)] * 2,
        compiler_params=_cp("parallel"),
        name="hyena_filter_mid",
    )(ffh, ffl, a, a, ksum)


def _merge_kernel(x_ref, at_ref, y_ref, z_ref, x0_ref, po_ref, g_ref, sk_ref, wg_ref, wa_ref, wh_ref,
                  wp_ref, wo_ref, o_ref):
    x = x_ref[0]
    h = _rms(x, g_ref[...]).astype(BF16)
    hy = ((y_ref[0] + z_ref[0] * sk_ref[...]) * x0_ref[0]).astype(BF16)
    merged = None
    for n, (br, w_ref) in enumerate(((at_ref[0], wa_ref), (hy, wh_ref), (po_ref[0], wp_ref))):
        gate = jax.nn.sigmoid(_dot(h, wg_ref[:, n * D_MODEL:(n + 1) * D_MODEL]))
        term = gate * _dot(br, w_ref[...])
        merged = term if merged is None else merged + term
    o_ref[0] = x + _dot(merged.astype(BF16), wo_ref[...])


def _merge(x, attn, y, z, x0, po, g, skip, wg, wa, wh, wp, wo):
    B, Lp, _ = x.shape
    tm = _row_tile(Lp, 704)
    row = lambda w: pl.BlockSpec((1, tm, w), lambda b, i: (b, i, 0))
    weights = (g, skip, wg, wa, wh, wp, wo)
    return pl.pallas_call(
        _merge_kernel,
        grid=(B, Lp // tm),
        in_specs=[row(D_MODEL), row(ATTN_W), row(HY_W), row(HY_W), row(HY_W), row(POOL_W)]
                 + [_const_spec(w.shape) for w in weights],
        out_specs=row(D_MODEL),
        out_shape=jax.ShapeDtypeStruct((B, Lp, D_MODEL), F32),
        compiler_params=_cp("parallel", "parallel"),
        name="merge",
    )(x, attn, y, z, x0, po, *weights)


def _ffn_kernel(Lp, tm, fc, xp_ref, x_ref, xn_ref, g_ref, wg_ref, wu_ref, cw_ref, cb_ref, wd_ref, o_ref):
    i = pl.program_id(1)
    ext = _with_halo(xp_ref, x_ref, xn_ref, i * tm, tm, Lp)
    h = _rms(ext, g_ref[...]).astype(BF16)
    hm = h[HALO:HALO + tm]
    acc = x_ref[0]
    for c in range(D_FF // fc):
        sl = slice(c * fc, (c + 1) * fc)
        gp = _dot(h, wg_ref[:, sl])
        cw = cw_ref[:, sl]
        conv = (cw[0:1] * _shift(gp, 1) + cw[1:2] * gp + cw[2:3] * _shift(gp, -1))[HALO:HALO + tm]
        conv = conv + cb_ref[:, sl]
        act = 0.5 * conv * (1.0 + lax.erf(conv * (1.0 / math.sqrt(2.0))))
        up = _dot(hm, wu_ref[:, sl])
        acc = acc + _dot((act * up).astype(BF16), wd_ref[sl, :])
    o_ref[0] = acc


def _ffn(x, g, wg, wu, cw, cb, wd):
    B, Lp, _ = x.shape
    tm = _row_tile(Lp, 704)
    fc = 1408
    main, prev, nxt = _halo_specs(tm, Lp, D_MODEL)
    weights = (g, wg, wu, cw, cb, wd)
    return pl.pallas_call(
        functools.partial(_ffn_kernel, Lp, tm, fc),
        grid=(B, Lp // tm),
        in_specs=[prev, main, nxt] + [_const_spec(w.shape) for w in weights],
        out_specs=main,
        out_shape=jax.ShapeDtypeStruct((B, Lp, D_MODEL), F32),
        compiler_params=_cp("parallel", "parallel"),
        name="ffn",
    )(x, x, x, *weights)


def _t5_buckets(rel):
    half = N_BUCKETS // 2
    ret = (rel > 0).astype(np.int32) * half
    n = np.abs(rel)
    max_exact = half // 2
    large = max_exact + (np.log(np.maximum(n, 1) / max_exact) / np.log(MAX_DIST / max_exact)
                         * (half - max_exact)).astype(np.int32)
    large = np.minimum(large, half - 1)
    return (ret + np.where(n < max_exact, n, large)).astype(np.int32)


def _bias_table(rel_bias):
    rel = np.arange(3 * BLK)[None, :] - BLK - np.arange(BLK)[:, None]
    band = np.abs(rel) <= WINDOW
    onehot = jnp.asarray(_t5_buckets(rel).reshape(-1)[None, :] == np.arange(N_BUCKETS)[:, None])
    onehot = onehot.astype(F32)
    bias = jnp.dot(rel_bias.astype(F32).T, onehot, precision=lax.Precision.HIGHEST)
    bias = jnp.where(band.reshape(1, -1), bias, NEG)
    return bias.reshape(N_HEADS * BLK, 3 * BLK)


def _dft_tables(n1, n1hp, nz):
    n = n1 * BLK
    n1h = n1 // 2 + 1

    k1 = np.arange(n1hp)
    a1 = 2.0 * np.pi * ((k1[:, None] * np.arange(nz)[None, :]) % n1) / n1
    a2 = 2.0 * np.pi * ((np.arange(BLK)[:, None] * k1[None, :]) % n) / n
    c1, s1, c2, s2 = (jnp.asarray(t, F32) for t in (np.cos(a1), np.sin(a1), np.cos(a2), np.sin(a2)))
    live = jnp.asarray(k1 < n1h, F32)
    w = jnp.asarray(np.where(k1 >= n1h, 0.0, np.where((k1 == 0) | (k1 == n1 // 2), 1.0, 2.0)) / n, F32)

    def cis(c2b, s2b, c1b, s1b, scale):
        return (c1b * c2b - s1b * s2b) * scale, -(s1b * c2b + c1b * s2b) * scale

    re, im = cis(c2[:, :, None], s2[:, :, None], c1[None], s1[None], live[None, :, None])
    m1 = jnp.concatenate([re, im], axis=1)
    re, im = cis(c2[:, None, :], s2[:, None, :], c1.T[None], s1.T[None], w[None, None, :])
    minv = jnp.concatenate([re, im], axis=2).astype(BF16)
    k2 = np.arange(BLK)
    a2 = 2.0 * np.pi * ((k2[:, None] * k2[None, :]) % BLK) / BLK
    fr, fi = np.cos(a2), -np.sin(a2)
    ff = _split(jnp.asarray(np.block([[fr, -fi], [fi, fr]]), F32))
    fv = jnp.asarray(np.block([[fr, fi], [-fi, fr]]), BF16)
    return _split(m1), minv, ff, fv


def _dup_heads(w):
    d = w.shape[0]
    w = w.reshape(d, N_KV, 1, HEAD_DIM)
    return jnp.broadcast_to(w, (d, N_KV, 2, HEAD_DIM)).reshape(d, 2 * KV_W)


def _geometry(L):
    Lp = _round_up(L + HALO, BLK)
    nz = _round_up(Lp // BLK, 16)
    n1 = _round_up(pl.cdiv(2 * L - 1, BLK), 2)
    n1h = n1 // 2 + 1
    n1hp = _round_up(n1h, 16)
    return Lp, nz, n1, n1h, n1hp


def _layer(x, L, geom, tabs, bias, p):
    Lp, nz, n1, n1h, n1hp = geom
    (m1h, m1l), minv, (ffh, ffl), fv = tabs
    q, k, v, u_hy, u_po = _inproj(x, p["norm_mix"], p["wq"], p["wk"], p["wv"], p["wh"], p["wp"],
                                  p["q_gain"], p["k_gain"], p["e"])
    attn = _attention(q, k, v, bias, p["sink"], L)
    po = _pool(u_po, p["pool_w"], p["pool_scale"], L)

    km, ksum = _filter_taps(L, nz, p["hy_w1"], p["hy_b1"], p["hy_w2"], p["hy_b2"], p["hy_w3"],
                            p["hy_freq"], p["deltas"])
    kr, ki = _kmid(_kfwd1(km, m1h, m1l), ksum, ffh, ffl)
    z, x0 = _hy_prep(u_hy, p["hy_conv_w"], p["hy_conv_b"], L, nz)
    gr, gi = _mid(_fwd1(z, m1h), kr, ki, ffh, fv, n1h)
    y = _inv1(gr, gi, minv)

    x = _merge(x, attn, y, z, x0, po, p["norm_mix"], p["hy_skip"], p["wgates"], p["w_attn_o"],
               p["w_hyena_o"], p["w_pool_o"], p["w_out"])
    return _ffn(x, p["norm_ffn"], p["w_gate"], p["w_up"], p["ffn_conv_w"], p["ffn_conv_b"],
                p["w_down"])


def _trunk(x, meta, bias, layers):
    B, S, _ = x.shape
    L = S + N_META
    geom = _geometry(L)
    Lp, nz, n1, n1h, n1hp = geom
    tabs = _dft_tables(n1, n1hp, nz)
    h = jnp.concatenate([jnp.broadcast_to(meta[None], (B, N_META, D_MODEL)), x,
                         jnp.zeros((B, Lp - L, D_MODEL), F32)], axis=1)
    for p in layers:
        h = _layer(h, L, geom, tabs, bias, p)
    return h[:, N_META:L]


def _prep_layers(norm_mix, w_in, q_gain, k_gain, attn_sink, hy_conv_w, hy_conv_b, hy_w1, hy_b1,
                 hy_w2, hy_b2, hy_w3, hy_freq, hy_skip, pool_w, pool_scale, w_attn_o, w_hyena_o,
                 w_pool_o, w_out, norm_ffn, w_gate, w_up, ffn_conv_w, ffn_conv_b, w_down):
    o1 = ATTN_W
    o2 = o1 + KV_W
    o3 = o2 + KV_W
    o4 = o3 + 3 * HY_W
    o5 = o4 + POOL_W
    lane = np.arange(LANE)
    e = jnp.asarray((lane[:, None] // HEAD_DIM) == (lane[None, :] // HEAD_DIM), BF16)
    deltas = jnp.abs(jnp.linspace(math.log(HY_DECAY_TARGET) / HY_FAST,
                                  math.log(HY_DECAY_TARGET) / HY_SLOW, HY_W, dtype=F32))[None]
    row = lambda a: a.astype(F32)[None]
    pad = lambda a, r, c: jnp.pad(a.astype(F32), ((0, r - a.shape[0]), (0, c - a.shape[1])))
    layers = []
    for l in range(norm_mix.shape[0]):
        w = w_in[l]
        layers.append(dict(
            norm_mix=row(norm_mix[l]),
            wq=w[:, :o1].astype(BF16),
            wk=_dup_heads(w[:, o1:o2]).astype(BF16),
            wv=_dup_heads(w[:, o2:o3]).astype(BF16),
            wh=w[:, o3:o4].astype(BF16),
            wp=w[:, o4:o5].astype(BF16),
            wgates=w[:, o5:].astype(BF16),
            q_gain=jnp.tile(row(q_gain[l]), (1, N_HEADS)) * (HEAD_DIM ** -0.5),
            k_gain=jnp.tile(row(k_gain[l]), (1, 2 * N_KV)),
            e=e,
            sink=attn_sink[l].astype(F32),
            hy_conv_w=hy_conv_w[l].astype(F32), hy_conv_b=row(hy_conv_b[l]),
            hy_w1=pad(hy_w1[l], LANE, LANE), hy_b1=pad(row(hy_b1[l]), 1, LANE),
            hy_w2=pad(hy_w2[l], LANE, LANE), hy_b2=pad(row(hy_b2[l]), 1, LANE),
            hy_w3=pad(hy_w3[l], LANE, 2 * HY_W), hy_freq=pad(row(hy_freq[l]), 1, LANE),
            hy_skip=row(hy_skip[l]), deltas=deltas,
            pool_w=pool_w[l].astype(BF16), pool_scale=row(pool_scale[l]),
            w_attn_o=w_attn_o[l].astype(BF16), w_hyena_o=w_hyena_o[l].astype(BF16),
            w_pool_o=w_pool_o[l].astype(BF16), w_out=w_out[l].astype(BF16),
            norm_ffn=row(norm_ffn[l]), w_gate=w_gate[l].astype(BF16), w_up=w_up[l].astype(BF16),
            ffn_conv_w=ffn_conv_w[l].astype(F32), ffn_conv_b=row(ffn_conv_b[l]),
            w_down=w_down[l].astype(BF16),
        ))
    return layers


def kernel(x_prompt, x_sample, meta_tokens, rel_bias, norm_mix, w_in, q_gain, k_gain, attn_sink,
           hy_conv_w, hy_conv_b, hy_w1, hy_b1, hy_w2, hy_b2, hy_w3, hy_freq, hy_skip, pool_w,
           pool_scale, w_attn_o, w_hyena_o, w_pool_o, w_out, norm_ffn, w_gate, w_up,
           ffn_conv_w, ffn_conv_b, w_down):
    layers = _prep_layers(norm_mix, w_in, q_gain, k_gain, attn_sink, hy_conv_w, hy_conv_b, hy_w1,
                          hy_b1, hy_w2, hy_b2, hy_w3, hy_freq, hy_skip, pool_w, pool_scale,
                          w_attn_o, w_hyena_o, w_pool_o, w_out, norm_ffn, w_gate, w_up,
                          ffn_conv_w, ffn_conv_b, w_down)
    bias = _bias_table(rel_bias)
    meta = meta_tokens.astype(F32)
    return (_trunk(x_prompt, meta, bias, layers), _trunk(x_sample, meta, bias, layers))
```

```python
import functools
import math

import numpy as np
import jax
import jax.numpy as jnp
from jax import lax
from jax.experimental import pallas as pl
from jax.experimental.pallas import tpu as pltpu

F32 = jnp.float32
BF16 = jnp.bfloat16

D_MODEL = 1024
N_META = 16
N_HEADS = 8
N_KV = 2
HEAD_DIM = 64
ATTN_W = N_HEADS * HEAD_DIM
KV_W = N_KV * HEAD_DIM
WINDOW = 128
BLK = 128
N_BUCKETS = 32
MAX_DIST = 128
HY_W = 512
HY_EMB = 33
HY_HID = 64
HY_DECAY_TARGET = 1e-2
HY_FAST = 0.3
HY_SLOW = 1.5
POOL_W = 512
POOL_WINDOWS = (2, 4, 8, 16)
POOL_G = 128
D_FF = 2816
EPS = 1e-6
NEG = -1e30
HALO = 8
LANE = 128
VMEM_LIMIT = 56 << 20


def _cp(*sem):
    return pltpu.CompilerParams(dimension_semantics=sem, vmem_limit_bytes=VMEM_LIMIT)


def _round_up(a, m):
    return -(-a // m) * m


def _row_tile(rows, cap, mult=16):
    best = None
    for t in range(mult, min(rows, cap) + 1, mult):
        if rows % t == 0:
            best = t
    assert best is not None, (rows, cap)
    return best


def _const_spec(shape):
    nd = len(shape)
    return pl.BlockSpec(shape, lambda *_: (0,) * nd, pipeline_mode=pl.Buffered(1))


def _dot(a, b):
    return jnp.dot(a, b, preferred_element_type=F32)


def _split(a):
    hi = a.astype(BF16)
    lo = (a - hi.astype(F32)).astype(BF16)
    return hi, lo


def _rms(x, g):
    ms = jnp.mean(x * x, axis=-1, keepdims=True)
    return x * lax.rsqrt(ms + EPS) * g


def _head_meansq(a, e):
    hi, lo = _split(a * a)
    cols = []
    for j in range(a.shape[1] // LANE):
        sl = slice(j * LANE, (j + 1) * LANE)
        cols.append(_dot(hi[:, sl], e) + _dot(lo[:, sl], e))
    return jnp.concatenate(cols, axis=1) * (1.0 / HEAD_DIM)


def _inproj_kernel(x_ref, g_ref, wq_ref, wk_ref, wv_ref, wh_ref, wp_ref, qg_ref, kg_ref, e_ref,
                   q_ref, k_ref, v_ref, hy_ref, po_ref):
    h = _rms(x_ref[0], g_ref[...]).astype(BF16)
    e = e_ref[...]
    q = _dot(h, wq_ref[...])
    q_ref[0] = (q * lax.rsqrt(_head_meansq(q, e) + EPS) * qg_ref[...]).astype(BF16)
    k = _dot(h, wk_ref[...])
    k_ref[0] = (k * lax.rsqrt(_head_meansq(k, e) + EPS) * kg_ref[...]).astype(BF16)
    v_ref[0] = _dot(h, wv_ref[...]).astype(BF16)
    hy_ref[0] = _dot(h, wh_ref[...])
    po_ref[0] = _dot(h, wp_ref[...])


def _inproj(x, g, wq, wk, wv, wh, wp, qg, kg, e):
    B, Lp, _ = x.shape
    tm = _row_tile(Lp, 704)
    row = lambda w: pl.BlockSpec((1, tm, w), lambda b, i: (b, i, 0))
    return pl.pallas_call(
        _inproj_kernel,
        grid=(B, Lp // tm),
        in_specs=[row(D_MODEL), _const_spec(g.shape), _const_spec(wq.shape), _const_spec(wk.shape),
                  _const_spec(wv.shape), _const_spec(wh.shape), _const_spec(wp.shape),
                  _const_spec(qg.shape), _const_spec(kg.shape), _const_spec(e.shape)],
        out_specs=[row(ATTN_W), row(2 * KV_W), row(2 * KV_W), row(3 * HY_W), row(POOL_W)],
        out_shape=[jax.ShapeDtypeStruct((B, Lp, ATTN_W), BF16),
                   jax.ShapeDtypeStruct((B, Lp, 2 * KV_W), BF16),
                   jax.ShapeDtypeStruct((B, Lp, 2 * KV_W), BF16),
                   jax.ShapeDtypeStruct((B, Lp, 3 * HY_W), F32),
                   jax.ShapeDtypeStruct((B, Lp, POOL_W), F32)],
        compiler_params=_cp("parallel", "parallel"),
        name="inproj",
    )(x, g, wq, wk, wv, wh, wp, qg, kg, e)


ATTN_ROWS = 32


def _attn_kernel(L, sink_ref, q_ref, kp_ref, ko_ref, kn_ref, vp_ref, vo_ref, vn_ref, bias_ref, o_ref,
                 s_scr, p_scr, r_scr):
    j = pl.program_id(1)
    q = q_ref[0]
    kwin = jnp.concatenate([kp_ref[0], ko_ref[0], kn_ref[0]], axis=0)
    vwin = jnp.concatenate([vp_ref[0], vo_ref[0], vn_ref[0]], axis=0)
    kpos = (j - 1) * BLK + lax.broadcasted_iota(jnp.int32, (1, 3 * BLK), 1)
    kvalid = (kpos >= 0) & (kpos < L)
    lane = lax.broadcasted_iota(jnp.int32, (BLK, LANE), 1)
    low = lane < HEAD_DIM
    zero = jnp.zeros((BLK, LANE), BF16)
    group = N_HEADS // N_KV
    outs = []
    for kv in range(N_KV):
        kk = kwin[:, kv * LANE:(kv + 1) * LANE]
        vv = vwin[:, kv * LANE:(kv + 1) * LANE]
        parts = []
        for c in range(2):
            qc = q[:, (2 * kv + c) * LANE:(2 * kv + c + 1) * LANE]
            parts += [jnp.where(low, qc, zero), jnp.where(low, zero, qc)]
        qm = jnp.concatenate(parts, axis=0)
        s_scr[...] = lax.dot_general(qm, kk, (((1,), (1,)), ((), ())), preferred_element_type=F32)
        for r0 in range(0, group * BLK, ATTN_ROWS):
            rows = slice(r0, r0 + ATTN_ROWS)
            s = s_scr[rows, :] + bias_ref[kv * group * BLK + r0:kv * group * BLK + r0 + ATTN_ROWS, :]
            s = jnp.where(kvalid, s, NEG)
            sink = sink_ref[kv * group + r0 // BLK]
            m = jnp.maximum(jnp.max(s, axis=-1, keepdims=True), sink)
            p = jnp.exp(s - m)
            denom = jnp.sum(p, axis=-1, keepdims=True) + jnp.exp(sink - m)
            p_scr[rows, :] = p.astype(BF16)
            r_scr[rows, :] = jnp.broadcast_to(1.0 / denom, (ATTN_ROWS, LANE))
        o = _dot(p_scr[...], vv) * r_scr[...]
        for c in range(2):
            outs.append(jnp.where(low, o[(2 * c) * BLK:(2 * c + 1) * BLK],
                                  o[(2 * c + 1) * BLK:(2 * c + 2) * BLK]))
    out = jnp.concatenate(outs, axis=1)
    qpos = j * BLK + lax.broadcasted_iota(jnp.int32, (BLK, 1), 0)
    o_ref[0] = jnp.where(qpos < L, out, 0.0).astype(BF16)


def _attention(q, k, v, bias, sink, L):
    B, Lp, _ = q.shape
    nb = Lp // BLK
    cur = lambda w: pl.BlockSpec((1, BLK, w), lambda b, j: (b, j, 0))
    prv = lambda w: pl.BlockSpec((1, BLK, w), lambda b, j: (b, jnp.maximum(j - 1, 0), 0))
    nxt = lambda w: pl.BlockSpec((1, BLK, w), lambda b, j: (b, jnp.minimum(j + 1, nb - 1), 0))
    kw = 2 * KV_W
    return pl.pallas_call(
        functools.partial(_attn_kernel, L),
        grid=(B, nb),
        in_specs=[pl.BlockSpec(memory_space=pltpu.SMEM), cur(ATTN_W), prv(kw), cur(kw), nxt(kw),
                  prv(kw), cur(kw), nxt(kw), _const_spec(bias.shape)],
        out_specs=cur(ATTN_W),
        out_shape=jax.ShapeDtypeStruct((B, Lp, ATTN_W), BF16),
        scratch_shapes=[pltpu.VMEM((N_HEADS // N_KV * BLK, 3 * BLK), F32),
                        pltpu.VMEM((N_HEADS // N_KV * BLK, 3 * BLK), BF16),
                        pltpu.VMEM((N_HEADS // N_KV * BLK, LANE), F32)],
        compiler_params=_cp("parallel", "parallel"),
        name="attention",
    )(sink, q, k, k, k, v, v, v, bias)


def _halo_specs(tm, rows, width):
    r8 = tm // HALO
    last = rows // HALO - 1
    main = pl.BlockSpec((1, tm, width), lambda b, i: (b, i, 0))
    prev = pl.BlockSpec((1, HALO, width), lambda b, i: (b, jnp.maximum(i * r8 - 1, 0), 0))
    nxt = pl.BlockSpec((1, HALO, width), lambda b, i: (b, jnp.minimum((i + 1) * r8, last), 0))
    return main, prev, nxt


def _with_halo(prev_ref, main_ref, next_ref, tile_start, tm, rows):
    pos = tile_start - HALO + lax.broadcasted_iota(jnp.int32, (tm + 2 * HALO, 1), 0)
    ext = jnp.concatenate([prev_ref[0], main_ref[0], next_ref[0]], axis=0)
    return jnp.where((pos >= 0) & (pos < rows), ext, 0.0)


def _shift(a, s):
    return pltpu.roll(a, s % a.shape[0], axis=0)


def _pool_kernel(L, Lp, tm, up_ref, u_ref, un_ref, w_ref, sc_ref, o_ref):
    i = pl.program_id(1)
    ext = _with_halo(up_ref, u_ref, un_ref, i * tm, tm, Lp)
    pos = i * tm + lax.broadcasted_iota(jnp.int32, (tm, 1), 0)
    outs = []
    for g, win in enumerate(POOL_WINDOWS):
        a = ext[:, g * POOL_G:(g + 1) * POOL_G]
        s = a + _shift(a, 1)
        half = 1
        while 2 * half < win:
            s = _shift(s, half) + _shift(s, -half)
            half *= 2
        lo = jnp.maximum(pos - win // 2, 0)
        hi = jnp.minimum(pos - win // 2 + win, L)
        cnt = jnp.maximum(hi - lo, 1).astype(F32)
        pooled = s[HALO:HALO + tm] / cnt - a[HALO:HALO + tm]
        pooled = jnp.where(pos < L, pooled, 0.0)
        outs.append(_dot(pooled.astype(BF16), w_ref[g]))
    o_ref[0] = (jnp.concatenate(outs, axis=1) * sc_ref[...]).astype(BF16)


def _pool(u, w, scale, L):
    B, Lp, _ = u.shape
    tm = _row_tile(Lp, 1408)
    main, prev, nxt = _halo_specs(tm, Lp, POOL_W)
    return pl.pallas_call(
        functools.partial(_pool_kernel, L, Lp, tm),
        grid=(B, Lp // tm),
        in_specs=[prev, main, nxt, _const_spec(w.shape), _const_spec(scale.shape)],
        out_specs=main,
        out_shape=jax.ShapeDtypeStruct((B, Lp, POOL_W), BF16),
        compiler_params=_cp("parallel", "parallel"),
        name="pool_mixer",
    )(u, u, u, w, scale)


def _hy_prep_kernel(L, Lp, tm, up_ref, u_ref, un_ref, cw_ref, cb_ref, z_ref, x0_ref):
    i = pl.program_id(1)
    ext = _with_halo(up_ref, u_ref, un_ref, i * tm, tm, Lp)
    cw = cw_ref[...]
    u = (cw[0:1] * _shift(ext, 1) + cw[1:2] * ext + cw[2:3] * _shift(ext, -1))[HALO:HALO + tm]
    u = u + cb_ref[...]
    pos = i * tm + lax.broadcasted_iota(jnp.int32, (tm, 1), 0)
    valid = pos < L
    z_ref[0] = jnp.where(valid, u[:, 2 * HY_W:] * u[:, HY_W:2 * HY_W], 0.0)
    x0_ref[0] = jnp.where(valid, u[:, :HY_W], 0.0)


def _hy_prep(u, cw, cb, L, nz):
    B, Lp, _ = u.shape
    tm = 1024
    rows = nz * BLK
    assert rows % tm == 0
    last_blk = pl.cdiv(Lp, tm) - 1
    last8 = Lp // HALO - 1
    r8 = tm // HALO
    main = pl.BlockSpec((1, tm, 3 * HY_W), lambda b, i: (b, jnp.minimum(i, last_blk), 0))
    prev = pl.BlockSpec((1, HALO, 3 * HY_W),
                        lambda b, i: (b, jnp.clip(i * r8 - 1, 0, last8), 0))
    nxt = pl.BlockSpec((1, HALO, 3 * HY_W),
                       lambda b, i: (b, jnp.minimum((i + 1) * r8, last8), 0))
    out = pl.BlockSpec((1, tm, HY_W), lambda b, i: (b, i, 0))
    return pl.pallas_call(
        functools.partial(_hy_prep_kernel, L, Lp, tm),
        grid=(B, rows // tm),
        in_specs=[prev, main, nxt, _const_spec(cw.shape), _const_spec(cb.shape)],
        out_specs=[out, out],
        out_shape=[jax.ShapeDtypeStruct((B, rows, HY_W), F32)] * 2,
        compiler_params=_cp("parallel", "parallel"),
        name="hyena_prep",
    )(u, u, u, cw, cb)


S2B = 16
FS2B = 8
KB = 8


def _lane_chunk_scratch(rows):
    return pltpu.VMEM((HY_W // LANE, rows, LANE), F32)


def _fill_chunks(scr, load_chunk):
    for c in range(HY_W // LANE):
        scr[c] = load_chunk(slice(c * LANE, (c + 1) * LANE)).astype(F32)


def _rows_at(scr, j, count, stride):
    return jnp.concatenate([scr[c, pl.ds(j, count, stride=stride), :]
                            for c in range(HY_W // LANE)], axis=1)


def _set_rows_at(scr, j, stride, val):
    for c in range(HY_W // LANE):
        scr[c, pl.ds(j, val.shape[0], stride=stride), :] = val[:, c * LANE:(c + 1) * LANE]


def _fwd1_kernel(m_ref, z_ref, a_ref, zs, outs):
    nz, two_h = z_ref.shape[1], m_ref.shape[1]
    _fill_chunks(zs, lambda ln: z_ref[0, :, :, ln].reshape(nz * S2B, LANE))
    for j in range(S2B):
        _set_rows_at(outs, j, S2B, _dot(m_ref[j], _rows_at(zs, j, nz, S2B).astype(BF16)))
    for c in range(HY_W // LANE):
        a_ref[0, :, :, c * LANE:(c + 1) * LANE] = outs[c].reshape(two_h, S2B, LANE).astype(BF16)


def _fwd1(z, m1):
    B, rows, _ = z.shape
    nz = rows // BLK
    two_h = m1.shape[1]
    blk = lambda r: pl.BlockSpec((1, r, S2B, HY_W), lambda b, s: (b, 0, s, 0))
    return pl.pallas_call(
        _fwd1_kernel,
        grid=(B, BLK // S2B),
        in_specs=[pl.BlockSpec((S2B, two_h, nz), lambda b, s: (s, 0, 0)), blk(nz)],
        out_specs=blk(two_h),
        out_shape=jax.ShapeDtypeStruct((B, two_h, BLK, HY_W), BF16),
        scratch_shapes=[_lane_chunk_scratch(nz * S2B), _lane_chunk_scratch(two_h * S2B)],
        compiler_params=_cp("parallel", "parallel"),
        name="hyena_fwd1",
    )(m1, z.reshape(B, nz, BLK, HY_W))


def _mid_kernel(n1h, ff_ref, fi_ref, ar_ref, ai_ref, kr_ref, ki_ref, gr_ref, gi_ref):
    k = pl.program_id(0)

    @pl.when(k * KB < n1h)
    def _():
        for t in range(KB):
            x = _dot(ff_ref[...], jnp.concatenate([ar_ref[0, t], ai_ref[0, t]], axis=0))
            xr, xi = x[:BLK], x[BLK:]
            kr, ki = kr_ref[t].astype(F32), ki_ref[t].astype(F32)
            y = jnp.concatenate([xr * kr - xi * ki, xr * ki + xi * kr], axis=0).astype(BF16)
            g = _dot(fi_ref[...], y)
            gr_ref[0, t] = g[:BLK].astype(BF16)
            gi_ref[0, t] = g[BLK:].astype(BF16)

    @pl.when(k * KB >= n1h)
    def _():
        gr_ref[...] = jnp.zeros_like(gr_ref)
        gi_ref[...] = jnp.zeros_like(gi_ref)


def _mid(a, kr, ki, ff, fi, n1h):
    B, two_h, _, _ = a.shape
    n1hp = two_h // 2
    assert n1hp % KB == 0
    blk = lambda off: pl.BlockSpec((1, KB, BLK, HY_W), lambda k, b: (b, k + off // KB, 0, 0))
    kblk = pl.BlockSpec((KB, BLK, HY_W), lambda k, b: (k, 0, 0))
    return pl.pallas_call(
        functools.partial(_mid_kernel, n1h),
        grid=(n1hp // KB, B),
        in_specs=[_const_spec(ff.shape), _const_spec(fi.shape), blk(0), blk(n1hp), kblk, kblk],
        out_specs=[blk(0), blk(0)],
        out_shape=[jax.ShapeDtypeStruct((B, n1hp, BLK, HY_W), BF16)] * 2,
        compiler_params=_cp("parallel", "parallel"),
        name="hyena_mid",
    )(ff, fi, a, a, kr, ki)


def _inv1_kernel(m_ref, gr_ref, gi_ref, y_ref, grs, gis, outs):
    n1hp, nz = gr_ref.shape[1], m_ref.shape[1]
    _fill_chunks(grs, lambda ln: gr_ref[0, :, :, ln].reshape(n1hp * S2B, LANE))
    _fill_chunks(gis, lambda ln: gi_ref[0, :, :, ln].reshape(n1hp * S2B, LANE))
    for j in range(S2B):
        g = jnp.concatenate([_rows_at(grs, j, n1hp, S2B), _rows_at(gis, j, n1hp, S2B)], axis=0)
        _set_rows_at(outs, j, S2B, _dot(m_ref[j], g.astype(BF16)))
    for c in range(HY_W // LANE):
        y_ref[0, :, :, c * LANE:(c + 1) * LANE] = outs[c].reshape(nz, S2B, LANE)


def _inv1(gr, gi, minv):
    B, n1hp, _, _ = gr.shape
    nz = minv.shape[1]
    blk = lambda r: pl.BlockSpec((1, r, S2B, HY_W), lambda b, s: (b, 0, s, 0))
    out = pl.pallas_call(
        _inv1_kernel,
        grid=(B, BLK // S2B),
        in_specs=[pl.BlockSpec((S2B, nz, 2 * n1hp), lambda b, s: (s, 0, 0)), blk(n1hp), blk(n1hp)],
        out_specs=blk(nz),
        out_shape=jax.ShapeDtypeStruct((B, nz, BLK, HY_W), F32),
        scratch_shapes=[_lane_chunk_scratch(n1hp * S2B), _lane_chunk_scratch(n1hp * S2B),
                        _lane_chunk_scratch(nz * S2B)],
        compiler_params=_cp("parallel", "parallel"),
        name="hyena_inv1",
    )(minv, gr, gi)
    return out.reshape(B, nz * BLK, HY_W)


def _filter_kernel(L, tm, w1_ref, b1_ref, w2_ref, b2_ref, w3_ref, fr_ref, dl_ref, k_ref, sum_ref,
                   cos_scr, sin_scr, dec_scr):
    i = pl.program_id(0)
    bands = (HY_EMB - 1) // 2
    lane = lax.broadcasted_iota(jnp.int32, (1, LANE), 1)
    band = jnp.where(lane <= bands, lane - 1, lane - 1 - bands)
    f = (1e-4 + ((bands - 1 - 1e-4) / (bands - 1)) * band.astype(F32)) * (2.0 * math.pi / L)
    r = lax.broadcasted_iota(jnp.int32, (tm, 1), 0).astype(F32)

    @pl.when(i == 0)
    def _():
        cos_scr[...] = jnp.cos(f * r)
        sin_scr[...] = jnp.sin(f * r)
        dec_scr[...] = jnp.exp(-(r * (1.0 / (L - 1))) * dl_ref[...])
        sum_ref[...] = jnp.zeros_like(sum_ref)

    m0 = (i * tm).astype(F32)
    ca, sa = jnp.cos(f * m0), jnp.sin(f * m0)
    is_cos = (lane >= 1) & (lane <= bands)
    is_sin = (lane > bands) & (lane < HY_EMB)
    pc = jnp.where(is_cos, ca, jnp.where(is_sin, -sa, 0.0))
    ps = jnp.where(is_cos, -sa, jnp.where(is_sin, -ca, 0.0))
    t = (m0 + r) * (1.0 / (L - 1))
    feat = pc * cos_scr[...] + ps * sin_scr[...] + jnp.where(lane == 0, t, 0.0)

    def dot3(a, w):
        (a_hi, a_lo), (w_hi, w_lo) = a, _split(w)
        return _dot(a_hi, w_hi) + _dot(a_hi, w_lo) + _dot(a_lo, w_hi)

    half = tm // 2
    fr = fr_ref[...]
    pre = dot3(_split(feat[:half]), w1_ref[0]) + dot3(_split(feat[half:]), w1_ref[1])
    h = jnp.sin(fr * (pre + b1_ref[...]))
    h = _split(jnp.sin(fr * (dot3(_split(h), w2_ref[...]) + b2_ref[...])))
    kf = jnp.concatenate([dot3(h, w3_ref[0]), dot3(h, w3_ref[1])], axis=0)
    decay = jnp.exp(-(m0 * (1.0 / (L - 1))) * dl_ref[...]) * dec_scr[...]
    m = i * tm + lax.broadcasted_iota(jnp.int32, (tm, 1), 0)
    col = lax.broadcasted_iota(jnp.int32, (1, 2 * HY_W), 1)
    keep = (m < L) & ((col < HY_W) | (m >= 1))
    k = jnp.where(keep, kf * jnp.concatenate([decay, decay], axis=1), 0.0)
    k_ref[...] = k
    sum_ref[...] += jnp.sum(jnp.abs(k).reshape(tm // 8, 8, 2 * HY_W), axis=0)


def _filter_taps(L, nz, w1, b1, w2, b2, w3, freq, deltas):
    rows = nz * BLK
    tm = _row_tile(rows, 1024, 8)
    args = (w1, b1, w2, b2, w3, freq, deltas)
    return pl.pallas_call(
        functools.partial(_filter_kernel, L, tm),
        grid=(rows // tm,),
        in_specs=[_const_spec(a.shape) for a in args],
        out_specs=[pl.BlockSpec((tm, 2 * HY_W), lambda i: (i, 0)),
                   pl.BlockSpec((8, 2 * HY_W), lambda i: (0, 0))],
        out_shape=[jax.ShapeDtypeStruct((rows, 2 * HY_W), F32),
                   jax.ShapeDtypeStruct((8, 2 * HY_W), F32)],
        scratch_shapes=[pltpu.VMEM((tm, LANE), F32), pltpu.VMEM((tm, LANE), F32),
                        pltpu.VMEM((tm, HY_W), F32)],
        compiler_params=_cp("arbitrary"),
        name="hyena_filter",
    )(*args)


def _kfwd1_kernel(n1hp, m_ref, mm_ref, mz_ref, kf_ref, kbm_ref, kbz_ref, a_ref, kfs, kbms, kbzs, outs):
    nz = kf_ref.shape[0]
    for scr, ref in ((kfs, kf_ref), (kbms, kbm_ref), (kbzs, kbz_ref)):
        _fill_chunks(scr, lambda ln, ref=ref: ref[:, :, ln].reshape(nz * FS2B, LANE))
    for t in range(FS2B):
        fwd = _dot(m_ref[t], _rows_at(kfs, t, nz, FS2B).astype(BF16))
        if t == 0:
            bwd = _dot(mz_ref[0], _rows_at(kbzs, 0, nz, FS2B).astype(BF16))
        else:
            bwd = _dot(mm_ref[FS2B - t], _rows_at(kbms, FS2B - t, nz, FS2B).astype(BF16))
        _set_rows_at(outs, t, FS2B, fwd + jnp.concatenate([bwd[:n1hp], -bwd[n1hp:]], axis=0))
    for c in range(HY_W // LANE):
        a_ref[:, :, c * LANE:(c + 1) * LANE] = outs[c].reshape(2 * n1hp, FS2B, LANE)


def _kfwd1(km, m1):
    rows = km.shape[0]
    nz = rows // BLK
    two_h = m1.shape[1]
    nblk = BLK // FS2B
    same = lambda i: i
    mirror = lambda i: nblk - 1 - i
    wrap = lambda i: (nblk - i) % nblk
    mat = lambda f: pl.BlockSpec((FS2B, two_h, nz), lambda i: (f(i), 0, 0))
    mat0 = pl.BlockSpec((1, two_h, nz), lambda i: (wrap(i) * FS2B, 0, 0))
    taps = lambda f, half: pl.BlockSpec((nz, FS2B, HY_W), lambda i: (0, f(i), half))
    return pl.pallas_call(
        functools.partial(_kfwd1_kernel, two_h // 2),
        grid=(nblk,),
        in_specs=[mat(same), mat(mirror), mat0, taps(same, 0), taps(mirror, 1), taps(wrap, 1)],
        out_specs=pl.BlockSpec((two_h, FS2B, HY_W), lambda i: (0, i, 0)),
        out_shape=jax.ShapeDtypeStruct((two_h, BLK, HY_W), F32),
        scratch_shapes=[_lane_chunk_scratch(nz * FS2B)] * 3 + [_lane_chunk_scratch(two_h * FS2B)],
        compiler_params=_cp("parallel"),
        name="hyena_filter_fwd1",
    )(m1, m1, m1, *([km.reshape(nz, BLK, 2 * HY_W)] * 3))


def _kmid_kernel(ff_ref, ar_ref, ai_ref, sum_ref, kr_ref, ki_ref):
    s = jnp.sum(sum_ref[...], axis=0, keepdims=True)
    inv = 1.0 / (s[:, :HY_W] + s[:, HY_W:])
    for t in range(KB):
        x = _dot(ff_ref[...], jnp.concatenate([ar_ref[t], ai_ref[t]], axis=0).astype(BF16))
        kr_ref[t] = (x[:BLK] * inv).astype(BF16)
        ki_ref[t] = (x[BLK:] * inv).astype(BF16)


def _kmid(a, ksum, ff):
    two_h = a.shape[0]
    n1hp = two_h // 2
    blk = lambda off: pl.BlockSpec((KB, BLK, HY_W), lambda k: (k + off // KB, 0, 0))
    return pl.pallas_call(
        _kmid_kernel,
        grid=(n1hp // KB,),
        in_specs=[_const_spec(ff.shape), blk(0), blk(n1hp), _const_spec(ksum.shape)],
        out_specs=[blk(0), blk(0)],
        out_shape=[jax.ShapeDtypeStruct((n1hp, BLK, HY_W), BF16)] * 2,
        compiler_params=_cp("parallel"),
        name="hyena_filter_mid",
    )(ff, a, a, ksum)


def _merge_kernel(x_ref, at_ref, y_ref, z_ref, x0_ref, po_ref, g_ref, sk_ref, wg_ref, wa_ref, wh_ref,
                  wp_ref, wo_ref, o_ref):
    x = x_ref[0]
    h = _rms(x, g_ref[...]).astype(BF16)
    hy = ((y_ref[0] + z_ref[0] * sk_ref[...]) * x0_ref[0]).astype(BF16)
    merged = None
    for n, (br, w_ref) in enumerate(((at_ref[0], wa_ref), (hy, wh_ref), (po_ref[0], wp_ref))):
        gate = jax.nn.sigmoid(_dot(h, wg_ref[:, n * D_MODEL:(n + 1) * D_MODEL]))
        term = gate * _dot(br, w_ref[...])
        merged = term if merged is None else merged + term
    o_ref[0] = x + _dot(merged.astype(BF16), wo_ref[...])


def _merge(x, attn, y, z, x0, po, g, skip, wg, wa, wh, wp, wo):
    B, Lp, _ = x.shape
    tm = _row_tile(Lp, 704)
    row = lambda w: pl.BlockSpec((1, tm, w), lambda b, i: (b, i, 0))
    weights = (g, skip, wg, wa, wh, wp, wo)
    return pl.pallas_call(
        _merge_kernel,
        grid=(B, Lp // tm),
        in_specs=[row(D_MODEL), row(ATTN_W), row(HY_W), row(HY_W), row(HY_W), row(POOL_W)]
                 + [_const_spec(w.shape) for w in weights],
        out_specs=row(D_MODEL),
        out_shape=jax.ShapeDtypeStruct((B, Lp, D_MODEL), F32),
        compiler_params=_cp("parallel", "parallel"),
        name="merge",
    )(x, attn, y, z, x0, po, *weights)


def _ffn_kernel(Lp, tm, fc, xp_ref, x_ref, xn_ref, g_ref, wg_ref, wu_ref, cw_ref, cb_ref, wd_ref, o_ref):
    i = pl.program_id(1)
    ext = _with_halo(xp_ref, x_ref, xn_ref, i * tm, tm, Lp)
    h = _rms(ext, g_ref[...]).astype(BF16)
    hm = h[HALO:HALO + tm]
    acc = x_ref[0]
    for c in range(D_FF // fc):
        sl = slice(c * fc, (c + 1) * fc)
        gp = _dot(h, wg_ref[:, sl])
        cw = cw_ref[:, sl]
        conv = (cw[0:1] * _shift(gp, 1) + cw[1:2] * gp + cw[2:3] * _shift(gp, -1))[HALO:HALO + tm]
        conv = conv + cb_ref[:, sl]
        act = 0.5 * conv * (1.0 + lax.erf(conv * (1.0 / math.sqrt(2.0))))
        up = _dot(hm, wu_ref[:, sl])
        acc = acc + _dot((act * up).astype(BF16), wd_ref[sl, :])
    o_ref[0] = acc


def _ffn(x, g, wg, wu, cw, cb, wd):
    B, Lp, _ = x.shape
    tm = _row_tile(Lp, 704)
    fc = 1408
    main, prev, nxt = _halo_specs(tm, Lp, D_MODEL)
    weights = (g, wg, wu, cw, cb, wd)
    return pl.pallas_call(
        functools.partial(_ffn_kernel, Lp, tm, fc),
        grid=(B, Lp // tm),
        in_specs=[prev, main, nxt] + [_const_spec(w.shape) for w in weights],
        out_specs=main,
        out_shape=jax.ShapeDtypeStruct((B, Lp, D_MODEL), F32),
        compiler_params=_cp("parallel", "parallel"),
        name="ffn",
    )(x, x, x, *weights)


def _t5_buckets(rel):
    half = N_BUCKETS // 2
    ret = (rel > 0).astype(np.int32) * half
    n = np.abs(rel)
    max_exact = half // 2
    large = max_exact + (np.log(np.maximum(n, 1) / max_exact) / np.log(MAX_DIST / max_exact)
                         * (half - max_exact)).astype(np.int32)
    large = np.minimum(large, half - 1)
    return (ret + np.where(n < max_exact, n, large)).astype(np.int32)


def _bias_table(rel_bias):
    rel = np.arange(3 * BLK)[None, :] - BLK - np.arange(BLK)[:, None]
    band = np.abs(rel) <= WINDOW
    onehot = jnp.asarray(_t5_buckets(rel).reshape(-1)[None, :] == np.arange(N_BUCKETS)[:, None])
    onehot = onehot.astype(F32)
    bias = jnp.dot(rel_bias.astype(F32).T, onehot, precision=lax.Precision.HIGHEST)
    bias = jnp.where(band.reshape(1, -1), bias, NEG)
    return bias.reshape(N_HEADS * BLK, 3 * BLK)


def _dft_tables(n1, n1hp, nz):
    n = n1 * BLK
    n1h = n1 // 2 + 1

    k1 = np.arange(n1hp)
    a1 = 2.0 * np.pi * ((k1[:, None] * np.arange(nz)[None, :]) % n1) / n1
    a2 = 2.0 * np.pi * ((np.arange(BLK)[:, None] * k1[None, :]) % n) / n
    c1, s1, c2, s2 = (jnp.asarray(t, F32) for t in (np.cos(a1), np.sin(a1), np.cos(a2), np.sin(a2)))
    live = jnp.asarray(k1 < n1h, F32)
    w = jnp.asarray(np.where(k1 >= n1h, 0.0, np.where((k1 == 0) | (k1 == n1 // 2), 1.0, 2.0)) / n, F32)

    def cis(c2b, s2b, c1b, s1b, scale):
        return (c1b * c2b - s1b * s2b) * scale, -(s1b * c2b + c1b * s2b) * scale

    re, im = cis(c2[:, :, None], s2[:, :, None], c1[None], s1[None], live[None, :, None])
    m1 = jnp.concatenate([re, im], axis=1).astype(BF16)
    re, im = cis(c2[:, None, :], s2[:, None, :], c1.T[None], s1.T[None], w[None, None, :])
    minv = jnp.concatenate([re, im], axis=2).astype(BF16)
    k2 = np.arange(BLK)
    a2 = 2.0 * np.pi * ((k2[:, None] * k2[None, :]) % BLK) / BLK
    fr, fi = np.cos(a2), -np.sin(a2)
    ff = jnp.asarray(np.block([[fr, -fi], [fi, fr]]), BF16)
    fv = jnp.asarray(np.block([[fr, fi], [-fi, fr]]), BF16)
    return m1, minv, ff, fv


def _dup_heads(w):
    d = w.shape[0]
    w = w.reshape(d, N_KV, 1, HEAD_DIM)
    return jnp.broadcast_to(w, (d, N_KV, 2, HEAD_DIM)).reshape(d, 2 * KV_W)


def _geometry(L):
    Lp = _round_up(L + HALO, BLK)
    nz = _round_up(Lp // BLK, 16)
    n1 = _round_up(pl.cdiv(2 * L - 1, BLK), 2)
    n1h = n1 // 2 + 1
    n1hp = _round_up(n1h, 16)
    return Lp, nz, n1, n1h, n1hp


def _layer(x, L, geom, tabs, bias, p):
    Lp, nz, n1, n1h, n1hp = geom
    m1, minv, ff, fv = tabs
    q, k, v, u_hy, u_po = _inproj(x, p["norm_mix"], p["wq"], p["wk"], p["wv"], p["wh"], p["wp"],
                                  p["q_gain"], p["k_gain"], p["e"])
    attn = _attention(q, k, v, bias, p["sink"], L)
    po = _pool(u_po, p["pool_w"], p["pool_scale"], L)

    km, ksum = _filter_taps(L, nz, p["hy_w1"], p["hy_b1"], p["hy_w2"], p["hy_b2"], p["hy_w3"],
                            p["hy_freq"], p["deltas"])
    kr, ki = _kmid(_kfwd1(km, m1), ksum, ff)
    z, x0 = _hy_prep(u_hy, p["hy_conv_w"], p["hy_conv_b"], L, nz)
    gr, gi = _mid(_fwd1(z, m1), kr, ki, ff, fv, n1h)
    y = _inv1(gr, gi, minv)

    x = _merge(x, attn, y, z, x0, po, p["norm_mix"], p["hy_skip"], p["wgates"], p["w_attn_o"],
               p["w_hyena_o"], p["w_pool_o"], p["w_out"])
    return _ffn(x, p["norm_ffn"], p["w_gate"], p["w_up"], p["ffn_conv_w"], p["ffn_conv_b"],
                p["w_down"])


def _trunk(x, meta, bias, layers):
    B, S, _ = x.shape
    L = S + N_META
    geom = _geometry(L)
    Lp, nz, n1, n1h, n1hp = geom
    tabs = _dft_tables(n1, n1hp, nz)
    h = jnp.concatenate([jnp.broadcast_to(meta[None], (B, N_META, D_MODEL)), x,
                         jnp.zeros((B, Lp - L, D_MODEL), F32)], axis=1)
    for p in layers:
        h = _layer(h, L, geom, tabs, bias, p)
    return h[:, N_META:L]


def _prep_layers(norm_mix, w_in, q_gain, k_gain, attn_sink, hy_conv_w, hy_conv_b, hy_w1, hy_b1,
                 hy_w2, hy_b2, hy_w3, hy_freq, hy_skip, pool_w, pool_scale, w_attn_o, w_hyena_o,
                 w_pool_o, w_out, norm_ffn, w_gate, w_up, ffn_conv_w, ffn_conv_b, w_down):
    o1 = ATTN_W
    o2 = o1 + KV_W
    o3 = o2 + KV_W
    o4 = o3 + 3 * HY_W
    o5 = o4 + POOL_W
    lane = np.arange(LANE)
    e = jnp.asarray((lane[:, None] // HEAD_DIM) == (lane[None, :] // HEAD_DIM), BF16)
    deltas = jnp.abs(jnp.linspace(math.log(HY_DECAY_TARGET) / HY_FAST,
                                  math.log(HY_DECAY_TARGET) / HY_SLOW, HY_W, dtype=F32))[None]
    row = lambda a: a.astype(F32)[None]


    def place(a, r0, c0, cols):
        return jnp.pad(a.astype(F32), ((r0, LANE - r0 - a.shape[0]), (c0, cols - c0 - a.shape[1])))

    layers = []
    for l in range(norm_mix.shape[0]):
        w = w_in[l]
        layers.append(dict(
            norm_mix=row(norm_mix[l]),
            wq=w[:, :o1].astype(BF16),
            wk=_dup_heads(w[:, o1:o2]).astype(BF16),
            wv=_dup_heads(w[:, o2:o3]).astype(BF16),
            wh=w[:, o3:o4].astype(BF16),
            wp=w[:, o4:o5].astype(BF16),
            wgates=w[:, o5:].astype(BF16),
            q_gain=jnp.tile(row(q_gain[l]), (1, N_HEADS)) * (HEAD_DIM ** -0.5),
            k_gain=jnp.tile(row(k_gain[l]), (1, 2 * N_KV)),
            e=e,
            sink=attn_sink[l].astype(F32),
            hy_conv_w=hy_conv_w[l].astype(F32), hy_conv_b=row(hy_conv_b[l]),
            hy_w1=jnp.stack([place(hy_w1[l], 0, 0, LANE), place(hy_w1[l], 0, HY_HID, LANE)]),
            hy_b1=jnp.tile(row(hy_b1[l]), (1, 2)),
            hy_w2=place(hy_w2[l], 0, 0, LANE) + place(hy_w2[l], HY_HID, HY_HID, LANE),
            hy_b2=jnp.tile(row(hy_b2[l]), (1, 2)),
            hy_w3=jnp.stack([place(hy_w3[l], 0, 0, 2 * HY_W), place(hy_w3[l], HY_HID, 0, 2 * HY_W)]),
            hy_freq=jnp.tile(row(hy_freq[l]), (1, 2)),
            hy_skip=row(hy_skip[l]), deltas=deltas,
            pool_w=pool_w[l].astype(BF16), pool_scale=row(pool_scale[l]),
            w_attn_o=w_attn_o[l].astype(BF16), w_hyena_o=w_hyena_o[l].astype(BF16),
            w_pool_o=w_pool_o[l].astype(BF16), w_out=w_out[l].astype(BF16),
            norm_ffn=row(norm_ffn[l]), w_gate=w_gate[l].astype(BF16), w_up=w_up[l].astype(BF16),
            ffn_conv_w=ffn_conv_w[l].astype(F32), ffn_conv_b=row(ffn_conv_b[l]),
            w_down=w_down[l].astype(BF16),
        ))
    return layers


def kernel(x_prompt, x_sample, meta_tokens, rel_bias, norm_mix, w_in, q_gain, k_gain, attn_sink,
           hy_conv_w, hy_conv_b, hy_w1, hy_b1, hy_w2, hy_b2, hy_w3, hy_freq, hy_skip, pool_w,
           pool_scale, w_attn_o, w_hyena_o, w_pool_o, w_out, norm_ffn, w_gate, w_up,
           ffn_conv_w, ffn_conv_b, w_down):
    layers = _prep_layers(norm_mix, w_in, q_gain, k_gain, attn_sink, hy_conv_w, hy_conv_b, hy_w1,
                          hy_b1, hy_w2, hy_b2, hy_w3, hy_freq, hy_skip, pool_w, pool_scale,
                          w_attn_o, w_hyena_o, w_pool_o, w_out, norm_ffn, w_gate, w_up,
                          ffn_conv_w, ffn_conv_b, w_down)
    bias = _bias_table(rel_bias)
    meta = meta_tokens.astype(F32)
    return (_trunk(x_prompt, meta, bias, layers), _trunk(x_sample, meta, bias, layers))
```

```python
import functools
import math

import numpy as np
import jax
import jax.numpy as jnp
from jax import lax
from jax.experimental import pallas as pl
from jax.experimental.pallas import tpu as pltpu

F32 = jnp.float32
BF16 = jnp.bfloat16

D_MODEL = 1024
N_META = 16
N_HEADS = 8
N_KV = 2
HEAD_DIM = 64
ATTN_W = N_HEADS * HEAD_DIM
KV_W = N_KV * HEAD_DIM
WINDOW = 128
BLK = 128
N_BUCKETS = 32
MAX_DIST = 128
HY_W = 512
HY_EMB = 33
HY_HID = 64
HY_DECAY_TARGET = 1e-2
HY_FAST = 0.3
HY_SLOW = 1.5
POOL_W = 512
POOL_WINDOWS = (2, 4, 8, 16)
POOL_G = 128
D_FF = 2816
EPS = 1e-6
NEG = -1e30
LOG2E = math.log2(math.e)
HALO = 8
LANE = 128
VMEM_LIMIT = 56 << 20


def _cp(*sem):
    return pltpu.CompilerParams(dimension_semantics=sem, vmem_limit_bytes=VMEM_LIMIT)


def _round_up(a, m):
    return -(-a // m) * m


def _row_tile(rows, cap, mult=16):
    best = None
    for t in range(mult, min(rows, cap) + 1, mult):
        if rows % t == 0:
            best = t
    assert best is not None, (rows, cap)
    return best


def _const_spec(shape):
    nd = len(shape)
    return pl.BlockSpec(shape, lambda *_: (0,) * nd, pipeline_mode=pl.Buffered(1))


def _dot(a, b):
    return jnp.dot(a, b, preferred_element_type=F32)


def _split(a):
    hi = a.astype(BF16)
    lo = (a - hi.astype(F32)).astype(BF16)
    return hi, lo


def _rms(x, g):
    ms = jnp.mean(x * x, axis=-1, keepdims=True)
    return x * lax.rsqrt(ms + EPS) * g


def _head_meansq(a, e):
    hi, lo = _split(a * a)
    cols = []
    for j in range(a.shape[1] // LANE):
        sl = slice(j * LANE, (j + 1) * LANE)
        cols.append(_dot(hi[:, sl], e) + _dot(lo[:, sl], e))
    return jnp.concatenate(cols, axis=1) * (1.0 / HEAD_DIM)


def _inproj_kernel(x_ref, g_ref, wq_ref, wk_ref, wv_ref, wh_ref, wp_ref, qg_ref, kg_ref, e_ref,
                   q_ref, k_ref, v_ref, hy_ref, po_ref):
    h = _rms(x_ref[0], g_ref[...]).astype(BF16)
    e = e_ref[...]
    q = _dot(h, wq_ref[...])
    q_ref[0] = (q * lax.rsqrt(_head_meansq(q, e) + EPS) * qg_ref[...]).astype(BF16)
    k = _dot(h, wk_ref[...])
    k_ref[0] = (k * lax.rsqrt(_head_meansq(k, e) + EPS) * kg_ref[...]).astype(BF16)
    v_ref[0] = _dot(h, wv_ref[...]).astype(BF16)
    hy_ref[0] = _dot(h, wh_ref[...])
    po_ref[0] = _dot(h, wp_ref[...])


def _inproj(x, g, wq, wk, wv, wh, wp, qg, kg, e):
    B, Lp, _ = x.shape
    tm = _row_tile(Lp, 704)
    row = lambda w: pl.BlockSpec((1, tm, w), lambda b, i: (b, i, 0))
    return pl.pallas_call(
        _inproj_kernel,
        grid=(B, Lp // tm),
        in_specs=[row(D_MODEL), _const_spec(g.shape), _const_spec(wq.shape), _const_spec(wk.shape),
                  _const_spec(wv.shape), _const_spec(wh.shape), _const_spec(wp.shape),
                  _const_spec(qg.shape), _const_spec(kg.shape), _const_spec(e.shape)],
        out_specs=[row(ATTN_W), row(2 * KV_W), row(2 * KV_W), row(3 * HY_W), row(POOL_W)],
        out_shape=[jax.ShapeDtypeStruct((B, Lp, ATTN_W), BF16),
                   jax.ShapeDtypeStruct((B, Lp, 2 * KV_W), BF16),
                   jax.ShapeDtypeStruct((B, Lp, 2 * KV_W), BF16),
                   jax.ShapeDtypeStruct((B, Lp, 3 * HY_W), F32),
                   jax.ShapeDtypeStruct((B, Lp, POOL_W), F32)],
        compiler_params=_cp("parallel", "parallel"),
        name="inproj",
    )(x, g, wq, wk, wv, wh, wp, qg, kg, e)


ATTN_ROWS = 32


def _attn_kernel(L, sink_ref, q_ref, kp_ref, ko_ref, kn_ref, vp_ref, vo_ref, vn_ref, bias_ref, o_ref,
                 s_scr, p_scr, r_scr):
    j = pl.program_id(1)
    q = q_ref[0]
    kwin = jnp.concatenate([kp_ref[0], ko_ref[0], kn_ref[0]], axis=0)
    vwin = jnp.concatenate([vp_ref[0], vo_ref[0], vn_ref[0]], axis=0)
    kpos = (j - 1) * BLK + lax.broadcasted_iota(jnp.int32, (1, 3 * BLK), 1)
    kvalid = (kpos >= 0) & (kpos < L)
    lane = lax.broadcasted_iota(jnp.int32, (BLK, LANE), 1)
    low = lane < HEAD_DIM
    zero = jnp.zeros((BLK, LANE), BF16)
    group = N_HEADS // N_KV
    outs = []
    for kv in range(N_KV):
        kk = kwin[:, kv * LANE:(kv + 1) * LANE]
        vv = vwin[:, kv * LANE:(kv + 1) * LANE]
        parts = []
        for c in range(2):
            qc = q[:, (2 * kv + c) * LANE:(2 * kv + c + 1) * LANE]
            parts += [jnp.where(low, qc, zero), jnp.where(low, zero, qc)]
        qm = jnp.concatenate(parts, axis=0)
        s_scr[...] = lax.dot_general(qm, kk, (((1,), (1,)), ((), ())), preferred_element_type=F32)

        for r0 in range(0, group * BLK, ATTN_ROWS):
            rows = slice(r0, r0 + ATTN_ROWS)
            s = s_scr[rows, :] + bias_ref[kv * group * BLK + r0:kv * group * BLK + r0 + ATTN_ROWS, :]
            s = jnp.where(kvalid, s, NEG)
            sink = sink_ref[kv * group + r0 // BLK]
            m = jnp.maximum(jnp.max(s, axis=-1, keepdims=True), sink)
            p = jnp.exp2(s - m)
            denom = jnp.sum(p, axis=-1, keepdims=True) + jnp.exp2(sink - m)
            p_scr[rows, :] = p.astype(BF16)
            r_scr[rows, :] = jnp.broadcast_to(1.0 / denom, (ATTN_ROWS, LANE))
        o = _dot(p_scr[...], vv) * r_scr[...]
        for c in range(2):
            outs.append(jnp.where(low, o[(2 * c) * BLK:(2 * c + 1) * BLK],
                                  o[(2 * c + 1) * BLK:(2 * c + 2) * BLK]))
    out = jnp.concatenate(outs, axis=1)
    qpos = j * BLK + lax.broadcasted_iota(jnp.int32, (BLK, 1), 0)
    o_ref[0] = jnp.where(qpos < L, out, 0.0).astype(BF16)


def _attention(q, k, v, bias, sink, L):
    B, Lp, _ = q.shape
    nb = Lp // BLK
    cur = lambda w: pl.BlockSpec((1, BLK, w), lambda b, j: (b, j, 0))
    prv = lambda w: pl.BlockSpec((1, BLK, w), lambda b, j: (b, jnp.maximum(j - 1, 0), 0))
    nxt = lambda w: pl.BlockSpec((1, BLK, w), lambda b, j: (b, jnp.minimum(j + 1, nb - 1), 0))
    kw = 2 * KV_W
    return pl.pallas_call(
        functools.partial(_attn_kernel, L),
        grid=(B, nb),
        in_specs=[pl.BlockSpec(memory_space=pltpu.SMEM), cur(ATTN_W), prv(kw), cur(kw), nxt(kw),
                  prv(kw), cur(kw), nxt(kw), _const_spec(bias.shape)],
        out_specs=cur(ATTN_W),
        out_shape=jax.ShapeDtypeStruct((B, Lp, ATTN_W), BF16),
        scratch_shapes=[pltpu.VMEM((N_HEADS // N_KV * BLK, 3 * BLK), F32),
                        pltpu.VMEM((N_HEADS // N_KV * BLK, 3 * BLK), BF16),
                        pltpu.VMEM((N_HEADS // N_KV * BLK, LANE), F32)],
        compiler_params=_cp("parallel", "parallel"),
        name="attention",
    )(sink, q, k, k, k, v, v, v, bias)


def _halo_specs(tm, rows, width):
    r8 = tm // HALO
    last = rows // HALO - 1
    main = pl.BlockSpec((1, tm, width), lambda b, i: (b, i, 0))
    prev = pl.BlockSpec((1, HALO, width), lambda b, i: (b, jnp.maximum(i * r8 - 1, 0), 0))
    nxt = pl.BlockSpec((1, HALO, width), lambda b, i: (b, jnp.minimum((i + 1) * r8, last), 0))
    return main, prev, nxt


def _with_halo(prev_ref, main_ref, next_ref, tile_start, tm, rows):
    pos = tile_start - HALO + lax.broadcasted_iota(jnp.int32, (tm + 2 * HALO, 1), 0)
    ext = jnp.concatenate([prev_ref[0], main_ref[0], next_ref[0]], axis=0)
    return jnp.where((pos >= 0) & (pos < rows), ext, 0.0)


def _shift(a, s):
    return pltpu.roll(a, s % a.shape[0], axis=0)


def _pool_kernel(L, Lp, tm, up_ref, u_ref, un_ref, w_ref, sc_ref, o_ref):
    i = pl.program_id(1)
    ext = _with_halo(up_ref, u_ref, un_ref, i * tm, tm, Lp)
    pos = i * tm + lax.broadcasted_iota(jnp.int32, (tm, 1), 0)
    outs = []
    for g, win in enumerate(POOL_WINDOWS):
        a = ext[:, g * POOL_G:(g + 1) * POOL_G]
        s = a + _shift(a, 1)
        half = 1
        while 2 * half < win:
            s = _shift(s, half) + _shift(s, -half)
            half *= 2
        lo = jnp.maximum(pos - win // 2, 0)
        hi = jnp.minimum(pos - win // 2 + win, L)
        cnt = jnp.maximum(hi - lo, 1).astype(F32)
        pooled = s[HALO:HALO + tm] / cnt - a[HALO:HALO + tm]
        pooled = jnp.where(pos < L, pooled, 0.0)
        outs.append(_dot(pooled.astype(BF16), w_ref[g]))
    o_ref[0] = (jnp.concatenate(outs, axis=1) * sc_ref[...]).astype(BF16)


def _pool(u, w, scale, L):
    B, Lp, _ = u.shape
    tm = _row_tile(Lp, 1408)
    main, prev, nxt = _halo_specs(tm, Lp, POOL_W)
    return pl.pallas_call(
        functools.partial(_pool_kernel, L, Lp, tm),
        grid=(B, Lp // tm),
        in_specs=[prev, main, nxt, _const_spec(w.shape), _const_spec(scale.shape)],
        out_specs=main,
        out_shape=jax.ShapeDtypeStruct((B, Lp, POOL_W), BF16),
        compiler_params=_cp("parallel", "parallel"),
        name="pool_mixer",
    )(u, u, u, w, scale)


def _hy_prep_kernel(L, Lp, tm, up_ref, u_ref, un_ref, cw_ref, cb_ref, z_ref, x0_ref):
    i = pl.program_id(1)
    ext = _with_halo(up_ref, u_ref, un_ref, i * tm, tm, Lp)
    cw = cw_ref[...]
    u = (cw[0:1] * _shift(ext, 1) + cw[1:2] * ext + cw[2:3] * _shift(ext, -1))[HALO:HALO + tm]
    u = u + cb_ref[...]
    pos = i * tm + lax.broadcasted_iota(jnp.int32, (tm, 1), 0)
    valid = pos < L
    z_ref[0] = jnp.where(valid, u[:, 2 * HY_W:] * u[:, HY_W:2 * HY_W], 0.0)
    x0_ref[0] = jnp.where(valid, u[:, :HY_W], 0.0)


def _hy_prep(u, cw, cb, L, nz):
    B, Lp, _ = u.shape
    tm = 1024
    rows = nz * BLK
    assert rows % tm == 0
    last_blk = pl.cdiv(Lp, tm) - 1
    last8 = Lp // HALO - 1
    r8 = tm // HALO
    main = pl.BlockSpec((1, tm, 3 * HY_W), lambda b, i: (b, jnp.minimum(i, last_blk), 0))
    prev = pl.BlockSpec((1, HALO, 3 * HY_W),
                        lambda b, i: (b, jnp.clip(i * r8 - 1, 0, last8), 0))
    nxt = pl.BlockSpec((1, HALO, 3 * HY_W),
                       lambda b, i: (b, jnp.minimum((i + 1) * r8, last8), 0))
    out = pl.BlockSpec((1, tm, HY_W), lambda b, i: (b, i, 0))
    return pl.pallas_call(
        functools.partial(_hy_prep_kernel, L, Lp, tm),
        grid=(B, rows // tm),
        in_specs=[prev, main, nxt, _const_spec(cw.shape), _const_spec(cb.shape)],
        out_specs=[out, out],
        out_shape=[jax.ShapeDtypeStruct((B, rows, HY_W), F32)] * 2,
        compiler_params=_cp("parallel", "parallel"),
        name="hyena_prep",
    )(u, u, u, cw, cb)


S2B = 16
FS2B = 8
KB = 8


def _lane_chunk_scratch(rows):
    return pltpu.VMEM((HY_W // LANE, rows, LANE), F32)


def _fill_chunks(scr, load_chunk):
    for c in range(HY_W // LANE):
        scr[c] = load_chunk(slice(c * LANE, (c + 1) * LANE)).astype(F32)


def _rows_at(scr, j, count, stride):
    return jnp.concatenate([scr[c, pl.ds(j, count, stride=stride), :]
                            for c in range(HY_W // LANE)], axis=1)


def _set_rows_at(scr, j, stride, val):
    for c in range(HY_W // LANE):
        scr[c, pl.ds(j, val.shape[0], stride=stride), :] = val[:, c * LANE:(c + 1) * LANE]


def _seqs_per_step(B):
    return max(d for d in (1, 2, 4) if B % d == 0)


def _fwd1_kernel(m_ref, z_ref, a_ref, zs, outs):
    bb, nzs = z_ref.shape[0], z_ref.shape[1]
    two_h, nzk = m_ref.shape[1], m_ref.shape[2]
    if nzk > nzs:
        zs[:, nzs * S2B:, :] = jnp.zeros((HY_W // LANE, (nzk - nzs) * S2B, LANE), F32)
    for b in range(bb):
        _fill_chunks(zs.at[:, :nzs * S2B], lambda ln: z_ref[b, :, :, ln].reshape(nzs * S2B, LANE))
        for j in range(S2B):
            _set_rows_at(outs, j, S2B, _dot(m_ref[j], _rows_at(zs, j, nzk, S2B).astype(BF16)))
        for c in range(HY_W // LANE):
            a_ref[b, :, :, c * LANE:(c + 1) * LANE] = outs[c].reshape(two_h, S2B, LANE).astype(BF16)


def _fwd1(z, m1):
    B, rows, _ = z.shape
    nzs = rows // BLK
    two_h, nzk = m1.shape[1], m1.shape[2]
    bb = _seqs_per_step(B)
    blk = lambda r: pl.BlockSpec((bb, r, S2B, HY_W), lambda b, s: (b, 0, s, 0))
    return pl.pallas_call(
        _fwd1_kernel,
        grid=(B // bb, BLK // S2B),
        in_specs=[pl.BlockSpec((S2B, two_h, nzk), lambda b, s: (s, 0, 0)), blk(nzs)],
        out_specs=blk(two_h),
        out_shape=jax.ShapeDtypeStruct((B, two_h, BLK, HY_W), BF16),
        scratch_shapes=[_lane_chunk_scratch(nzk * S2B), _lane_chunk_scratch(two_h * S2B)],
        compiler_params=_cp("parallel", "parallel"),
        name="hyena_fwd1",
    )(m1, z.reshape(B, nzs, BLK, HY_W))


def _mid_kernel(n1h, ff_ref, fi_ref, ar_ref, ai_ref, kr_ref, ki_ref, gr_ref, gi_ref):
    k = pl.program_id(0)

    @pl.when(k * KB < n1h)
    def _():
        for t in range(KB):
            x = _dot(ff_ref[...], jnp.concatenate([ar_ref[0, t], ai_ref[0, t]], axis=0))
            xr, xi = x[:BLK], x[BLK:]
            kr, ki = kr_ref[t].astype(F32), ki_ref[t].astype(F32)
            y = jnp.concatenate([xr * kr - xi * ki, xr * ki + xi * kr], axis=0).astype(BF16)
            g = _dot(fi_ref[...], y)
            gr_ref[0, t] = g[:BLK].astype(BF16)
            gi_ref[0, t] = g[BLK:].astype(BF16)

    @pl.when(k * KB >= n1h)
    def _():
        gr_ref[...] = jnp.zeros_like(gr_ref)
        gi_ref[...] = jnp.zeros_like(gi_ref)


def _mid(a, kr, ki, ff, fi, n1h):
    B, two_h, _, _ = a.shape
    n1hp = two_h // 2
    assert n1hp % KB == 0
    blk = lambda off: pl.BlockSpec((1, KB, BLK, HY_W), lambda k, b: (b, k + off // KB, 0, 0))
    kblk = pl.BlockSpec((KB, BLK, HY_W), lambda k, b: (k, 0, 0))
    return pl.pallas_call(
        functools.partial(_mid_kernel, n1h),
        grid=(n1hp // KB, B),
        in_specs=[_const_spec(ff.shape), _const_spec(fi.shape), blk(0), blk(n1hp), kblk, kblk],
        out_specs=[blk(0), blk(0)],
        out_shape=[jax.ShapeDtypeStruct((B, n1hp, BLK, HY_W), BF16)] * 2,
        compiler_params=_cp("parallel", "parallel"),
        name="hyena_mid",
    )(ff, fi, a, a, kr, ki)


def _inv1_kernel(m_ref, gr_ref, gi_ref, y_ref, grs, gis, outs):
    bb, n1hp, nzs = gr_ref.shape[0], gr_ref.shape[1], y_ref.shape[1]
    for b in range(bb):
        _fill_chunks(grs, lambda ln: gr_ref[b, :, :, ln].reshape(n1hp * S2B, LANE))
        _fill_chunks(gis, lambda ln: gi_ref[b, :, :, ln].reshape(n1hp * S2B, LANE))
        for j in range(S2B):
            g = jnp.concatenate([_rows_at(grs, j, n1hp, S2B), _rows_at(gis, j, n1hp, S2B)], axis=0)
            _set_rows_at(outs, j, S2B, _dot(m_ref[j], g.astype(BF16))[:nzs])
        for c in range(HY_W // LANE):
            y_ref[b, :, :, c * LANE:(c + 1) * LANE] = outs[c].reshape(nzs, S2B, LANE)


def _inv1(gr, gi, minv, nzs):
    B, n1hp, _, _ = gr.shape
    nzk = minv.shape[1]
    bb = _seqs_per_step(B)
    blk = lambda r: pl.BlockSpec((bb, r, S2B, HY_W), lambda b, s: (b, 0, s, 0))
    out = pl.pallas_call(
        _inv1_kernel,
        grid=(B // bb, BLK // S2B),
        in_specs=[pl.BlockSpec((S2B, nzk, 2 * n1hp), lambda b, s: (s, 0, 0)), blk(n1hp), blk(n1hp)],
        out_specs=blk(nzs),
        out_shape=jax.ShapeDtypeStruct((B, nzs, BLK, HY_W), F32),
        scratch_shapes=[_lane_chunk_scratch(n1hp * S2B), _lane_chunk_scratch(n1hp * S2B),
                        _lane_chunk_scratch(nzs * S2B)],
        compiler_params=_cp("parallel", "parallel"),
        name="hyena_inv1",
    )(minv, gr, gi)
    return out.reshape(B, nzs * BLK, HY_W)


def _filter_kernel(L, tm, w1_ref, b1_ref, w2_ref, b2_ref, w3_ref, fr_ref, dl_ref, k_ref, sum_ref,
                   cos_scr, sin_scr, dec_scr):
    i = pl.program_id(0)
    bands = (HY_EMB - 1) // 2
    lane = lax.broadcasted_iota(jnp.int32, (1, LANE), 1)
    band = jnp.where(lane <= bands, lane - 1, lane - 1 - bands)
    f = (1e-4 + ((bands - 1 - 1e-4) / (bands - 1)) * band.astype(F32)) * (2.0 * math.pi / L)
    r = lax.broadcasted_iota(jnp.int32, (tm, 1), 0).astype(F32)

    @pl.when(i == 0)
    def _():
        cos_scr[...] = jnp.cos(f * r)
        sin_scr[...] = jnp.sin(f * r)
        dec_scr[...] = jnp.exp(-(r * (1.0 / (L - 1))) * dl_ref[...])
        sum_ref[...] = jnp.zeros_like(sum_ref)

    m0 = (i * tm).astype(F32)
    ca, sa = jnp.cos(f * m0), jnp.sin(f * m0)
    is_cos = (lane >= 1) & (lane <= bands)
    is_sin = (lane > bands) & (lane < HY_EMB)
    pc = jnp.where(is_cos, ca, jnp.where(is_sin, -sa, 0.0))
    ps = jnp.where(is_cos, -sa, jnp.where(is_sin, -ca, 0.0))
    t = (m0 + r) * (1.0 / (L - 1))
    feat = pc * cos_scr[...] + ps * sin_scr[...] + jnp.where(lane == 0, t, 0.0)

    def dot3(a, w):
        (a_hi, a_lo), (w_hi, w_lo) = a, _split(w)
        return _dot(a_hi, w_hi) + _dot(a_hi, w_lo) + _dot(a_lo, w_hi)

    half = tm // 2
    fr = fr_ref[...]
    pre = dot3(_split(feat[:half]), w1_ref[0]) + dot3(_split(feat[half:]), w1_ref[1])
    h = jnp.sin(fr * (pre + b1_ref[...]))
    h = _split(jnp.sin(fr * (dot3(_split(h), w2_ref[...]) + b2_ref[...])))
    kf = jnp.concatenate([dot3(h, w3_ref[0]), dot3(h, w3_ref[1])], axis=0)
    decay = jnp.exp(-(m0 * (1.0 / (L - 1))) * dl_ref[...]) * dec_scr[...]
    m = i * tm + lax.broadcasted_iota(jnp.int32, (tm, 1), 0)
    col = lax.broadcasted_iota(jnp.int32, (1, 2 * HY_W), 1)
    keep = (m < L) & ((col < HY_W) | (m >= 1))
    k = jnp.where(keep, kf * jnp.concatenate([decay, decay], axis=1), 0.0)
    k_ref[...] = k
    sum_ref[...] += jnp.sum(jnp.abs(k).reshape(tm // 8, 8, 2 * HY_W), axis=0)


def _filter_taps(L, nz, w1, b1, w2, b2, w3, freq, deltas):
    rows = nz * BLK
    tm = _row_tile(rows, 1024, 8)
    args = (w1, b1, w2, b2, w3, freq, deltas)
    return pl.pallas_call(
        functools.partial(_filter_kernel, L, tm),
        grid=(rows // tm,),
        in_specs=[_const_spec(a.shape) for a in args],
        out_specs=[pl.BlockSpec((tm, 2 * HY_W), lambda i: (i, 0)),
                   pl.BlockSpec((8, 2 * HY_W), lambda i: (0, 0))],
        out_shape=[jax.ShapeDtypeStruct((rows, 2 * HY_W), F32),
                   jax.ShapeDtypeStruct((8, 2 * HY_W), F32)],
        scratch_shapes=[pltpu.VMEM((tm, LANE), F32), pltpu.VMEM((tm, LANE), F32),
                        pltpu.VMEM((tm, HY_W), F32)],
        compiler_params=_cp("arbitrary"),
        name="hyena_filter",
    )(*args)


def _kfwd1_kernel(n1hp, m_ref, mm_ref, mz_ref, kf_ref, kbm_ref, kbz_ref, a_ref, kfs, kbms, kbzs, outs):
    nz = kf_ref.shape[0]
    for scr, ref in ((kfs, kf_ref), (kbms, kbm_ref), (kbzs, kbz_ref)):
        _fill_chunks(scr, lambda ln, ref=ref: ref[:, :, ln].reshape(nz * FS2B, LANE))
    for t in range(FS2B):
        fwd = _dot(m_ref[t], _rows_at(kfs, t, nz, FS2B).astype(BF16))
        if t == 0:
            bwd = _dot(mz_ref[0], _rows_at(kbzs, 0, nz, FS2B).astype(BF16))
        else:
            bwd = _dot(mm_ref[FS2B - t], _rows_at(kbms, FS2B - t, nz, FS2B).astype(BF16))
        _set_rows_at(outs, t, FS2B, fwd + jnp.concatenate([bwd[:n1hp], -bwd[n1hp:]], axis=0))
    for c in range(HY_W // LANE):
        a_ref[:, :, c * LANE:(c + 1) * LANE] = outs[c].reshape(2 * n1hp, FS2B, LANE)


def _kfwd1(km, m1):
    rows = km.shape[0]
    nz = rows // BLK
    two_h = m1.shape[1]
    nblk = BLK // FS2B
    same = lambda i: i
    mirror = lambda i: nblk - 1 - i
    wrap = lambda i: (nblk - i) % nblk
    mat = lambda f: pl.BlockSpec((FS2B, two_h, nz), lambda i: (f(i), 0, 0))
    mat0 = pl.BlockSpec((1, two_h, nz), lambda i: (wrap(i) * FS2B, 0, 0))
    taps = lambda f, half: pl.BlockSpec((nz, FS2B, HY_W), lambda i: (0, f(i), half))
    return pl.pallas_call(
        functools.partial(_kfwd1_kernel, two_h // 2),
        grid=(nblk,),
        in_specs=[mat(same), mat(mirror), mat0, taps(same, 0), taps(mirror, 1), taps(wrap, 1)],
        out_specs=pl.BlockSpec((two_h, FS2B, HY_W), lambda i: (0, i, 0)),
        out_shape=jax.ShapeDtypeStruct((two_h, BLK, HY_W), F32),
        scratch_shapes=[_lane_chunk_scratch(nz * FS2B)] * 3 + [_lane_chunk_scratch(two_h * FS2B)],
        compiler_params=_cp("parallel"),
        name="hyena_filter_fwd1",
    )(m1, m1, m1, *([km.reshape(nz, BLK, 2 * HY_W)] * 3))


def _kmid_kernel(ff_ref, ar_ref, ai_ref, sum_ref, kr_ref, ki_ref):
    s = jnp.sum(sum_ref[...], axis=0, keepdims=True)
    inv = 1.0 / (s[:, :HY_W] + s[:, HY_W:])
    for t in range(KB):
        x = _dot(ff_ref[...], jnp.concatenate([ar_ref[t], ai_ref[t]], axis=0).astype(BF16))
        kr_ref[t] = (x[:BLK] * inv).astype(BF16)
        ki_ref[t] = (x[BLK:] * inv).astype(BF16)


def _kmid(a, ksum, ff):
    two_h = a.shape[0]
    n1hp = two_h // 2
    blk = lambda off: pl.BlockSpec((KB, BLK, HY_W), lambda k: (k + off // KB, 0, 0))
    return pl.pallas_call(
        _kmid_kernel,
        grid=(n1hp // KB,),
        in_specs=[_const_spec(ff.shape), blk(0), blk(n1hp), _const_spec(ksum.shape)],
        out_specs=[blk(0), blk(0)],
        out_shape=[jax.ShapeDtypeStruct((n1hp, BLK, HY_W), BF16)] * 2,
        compiler_params=_cp("parallel"),
        name="hyena_filter_mid",
    )(ff, a, a, ksum)


def _merge_kernel(x_ref, at_ref, y_ref, z_ref, x0_ref, po_ref, g_ref, sk_ref, wg_ref, wa_ref, wh_ref,
                  wp_ref, wo_ref, o_ref):
    x = x_ref[0]
    h = _rms(x, g_ref[...]).astype(BF16)
    hy = ((y_ref[0] + z_ref[0] * sk_ref[...]) * x0_ref[0]).astype(BF16)
    merged = None
    for n, (br, w_ref) in enumerate(((at_ref[0], wa_ref), (hy, wh_ref), (po_ref[0], wp_ref))):
        gate = jax.nn.sigmoid(_dot(h, wg_ref[:, n * D_MODEL:(n + 1) * D_MODEL]))
        term = gate * _dot(br, w_ref[...])
        merged = term if merged is None else merged + term
    o_ref[0] = x + _dot(merged.astype(BF16), wo_ref[...])


def _merge(x, attn, y, z, x0, po, g, skip, wg, wa, wh, wp, wo):
    B, Lp, _ = x.shape
    tm = _row_tile(Lp, 704)
    row = lambda w: pl.BlockSpec((1, tm, w), lambda b, i: (b, i, 0))
    weights = (g, skip, wg, wa, wh, wp, wo)
    return pl.pallas_call(
        _merge_kernel,
        grid=(B, Lp // tm),
        in_specs=[row(D_MODEL), row(ATTN_W), row(HY_W), row(HY_W), row(HY_W), row(POOL_W)]
                 + [_const_spec(w.shape) for w in weights],
        out_specs=row(D_MODEL),
        out_shape=jax.ShapeDtypeStruct((B, Lp, D_MODEL), F32),
        compiler_params=_cp("parallel", "parallel"),
        name="merge",
    )(x, attn, y, z, x0, po, *weights)


def _ffn_kernel(Lp, tm, fc, xp_ref, x_ref, xn_ref, g_ref, wg_ref, wu_ref, cw_ref, cb_ref, wd_ref, o_ref):
    i = pl.program_id(1)
    ext = _with_halo(xp_ref, x_ref, xn_ref, i * tm, tm, Lp)
    h = _rms(ext, g_ref[...]).astype(BF16)
    hm = h[HALO:HALO + tm]
    acc = x_ref[0]
    for c in range(D_FF // fc):
        sl = slice(c * fc, (c + 1) * fc)
        gp = _dot(h, wg_ref[:, sl])
        cw = cw_ref[:, sl]
        conv = (cw[0:1] * _shift(gp, 1) + cw[1:2] * gp + cw[2:3] * _shift(gp, -1))[HALO:HALO + tm]
        conv = conv + cb_ref[:, sl]
        act = 0.5 * conv * (1.0 + lax.erf(conv * (1.0 / math.sqrt(2.0))))
        up = _dot(hm, wu_ref[:, sl])
        acc = acc + _dot((act * up).astype(BF16), wd_ref[sl, :])
    o_ref[0] = acc


def _ffn(x, g, wg, wu, cw, cb, wd):
    B, Lp, _ = x.shape
    tm = _row_tile(Lp, 704)
    fc = 1408
    main, prev, nxt = _halo_specs(tm, Lp, D_MODEL)
    weights = (g, wg, wu, cw, cb, wd)
    return pl.pallas_call(
        functools.partial(_ffn_kernel, Lp, tm, fc),
        grid=(B, Lp // tm),
        in_specs=[prev, main, nxt] + [_const_spec(w.shape) for w in weights],
        out_specs=main,
        out_shape=jax.ShapeDtypeStruct((B, Lp, D_MODEL), F32),
        compiler_params=_cp("parallel", "parallel"),
        name="ffn",
    )(x, x, x, *weights)


def _t5_buckets(rel):
    half = N_BUCKETS // 2
    ret = (rel > 0).astype(np.int32) * half
    n = np.abs(rel)
    max_exact = half // 2
    large = max_exact + (np.log(np.maximum(n, 1) / max_exact) / np.log(MAX_DIST / max_exact)
                         * (half - max_exact)).astype(np.int32)
    large = np.minimum(large, half - 1)
    return (ret + np.where(n < max_exact, n, large)).astype(np.int32)


def _bias_table(rel_bias):
    rel = np.arange(3 * BLK)[None, :] - BLK - np.arange(BLK)[:, None]
    band = np.abs(rel) <= WINDOW
    onehot = jnp.asarray(_t5_buckets(rel).reshape(-1)[None, :] == np.arange(N_BUCKETS)[:, None])
    onehot = onehot.astype(F32)
    bias = jnp.dot(rel_bias.astype(F32).T, onehot, precision=lax.Precision.HIGHEST)
    bias = jnp.where(band.reshape(1, -1), bias * LOG2E, NEG)
    return bias.reshape(N_HEADS * BLK, 3 * BLK)


def _dft_tables(n1, n1hp, nz):
    n = n1 * BLK
    n1h = n1 // 2 + 1

    k1 = np.arange(n1hp)
    a1 = 2.0 * np.pi * ((k1[:, None] * np.arange(nz)[None, :]) % n1) / n1
    a2 = 2.0 * np.pi * ((np.arange(BLK)[:, None] * k1[None, :]) % n) / n
    c1, s1, c2, s2 = (jnp.asarray(t, F32) for t in (np.cos(a1), np.sin(a1), np.cos(a2), np.sin(a2)))
    live = jnp.asarray(k1 < n1h, F32)
    w = jnp.asarray(np.where(k1 >= n1h, 0.0, np.where((k1 == 0) | (k1 == n1 // 2), 1.0, 2.0)) / n, F32)

    def cis(c2b, s2b, c1b, s1b, scale):
        return (c1b * c2b - s1b * s2b) * scale, -(s1b * c2b + c1b * s2b) * scale

    re, im = cis(c2[:, :, None], s2[:, :, None], c1[None], s1[None], live[None, :, None])
    m1 = jnp.concatenate([re, im], axis=1).astype(BF16)
    re, im = cis(c2[:, None, :], s2[:, None, :], c1.T[None], s1.T[None], w[None, None, :])
    minv = jnp.concatenate([re, im], axis=2).astype(BF16)
    k2 = np.arange(BLK)
    a2 = 2.0 * np.pi * ((k2[:, None] * k2[None, :]) % BLK) / BLK
    fr, fi = np.cos(a2), -np.sin(a2)
    ff = jnp.asarray(np.block([[fr, -fi], [fi, fr]]), BF16)
    fv = jnp.asarray(np.block([[fr, fi], [-fi, fr]]), BF16)
    return m1, minv, ff, fv


def _dup_heads(w):
    d = w.shape[0]
    w = w.reshape(d, N_KV, 1, HEAD_DIM)
    return jnp.broadcast_to(w, (d, N_KV, 2, HEAD_DIM)).reshape(d, 2 * KV_W)


def _geometry(L):
    Lp = _round_up(L + HALO, BLK)
    nzs = _round_up(Lp // BLK, 8)
    nzk = _round_up(nzs, 16)
    n1 = _round_up(pl.cdiv(2 * L - 1, BLK), 2)
    n1h = n1 // 2 + 1
    n1hp = _round_up(n1h, 8)
    return Lp, nzs, nzk, n1, n1h, n1hp


def _layer(x, L, geom, tabs, bias, p):
    Lp, nzs, nzk, n1, n1h, n1hp = geom
    m1, minv, ff, fv = tabs
    q, k, v, u_hy, u_po = _inproj(x, p["norm_mix"], p["wq"], p["wk"], p["wv"], p["wh"], p["wp"],
                                  p["q_gain"], p["k_gain"], p["e"])
    attn = _attention(q, k, v, bias, p["sink"], L)
    po = _pool(u_po, p["pool_w"], p["pool_scale"], L)

    km, ksum = _filter_taps(L, nzk, p["hy_w1"], p["hy_b1"], p["hy_w2"], p["hy_b2"], p["hy_w3"],
                            p["hy_freq"], p["deltas"])
    kr, ki = _kmid(_kfwd1(km, m1), ksum, ff)
    z, x0 = _hy_prep(u_hy, p["hy_conv_w"], p["hy_conv_b"], L, nzs)
    gr, gi = _mid(_fwd1(z, m1), kr, ki, ff, fv, n1h)
    y = _inv1(gr, gi, minv, nzs)

    x = _merge(x, attn, y, z, x0, po, p["norm_mix"], p["hy_skip"], p["wgates"], p["w_attn_o"],
               p["w_hyena_o"], p["w_pool_o"], p["w_out"])
    return _ffn(x, p["norm_ffn"], p["w_gate"], p["w_up"], p["ffn_conv_w"], p["ffn_conv_b"],
                p["w_down"])


def _trunk(x, meta, bias, layers):
    B, S, _ = x.shape
    L = S + N_META
    geom = _geometry(L)
    Lp, nzs, nzk, n1, n1h, n1hp = geom
    tabs = _dft_tables(n1, n1hp, nzk)
    h = jnp.concatenate([jnp.broadcast_to(meta[None], (B, N_META, D_MODEL)), x,
                         jnp.zeros((B, Lp - L, D_MODEL), F32)], axis=1)
    for p in layers:
        h = _layer(h, L, geom, tabs, bias, p)
    return h[:, N_META:L]


def _prep_layers(norm_mix, w_in, q_gain, k_gain, attn_sink, hy_conv_w, hy_conv_b, hy_w1, hy_b1,
                 hy_w2, hy_b2, hy_w3, hy_freq, hy_skip, pool_w, pool_scale, w_attn_o, w_hyena_o,
                 w_pool_o, w_out, norm_ffn, w_gate, w_up, ffn_conv_w, ffn_conv_b, w_down):
    o1 = ATTN_W
    o2 = o1 + KV_W
    o3 = o2 + KV_W
    o4 = o3 + 3 * HY_W
    o5 = o4 + POOL_W
    lane = np.arange(LANE)
    e = jnp.asarray((lane[:, None] // HEAD_DIM) == (lane[None, :] // HEAD_DIM), BF16)
    deltas = jnp.abs(jnp.linspace(math.log(HY_DECAY_TARGET) / HY_FAST,
                                  math.log(HY_DECAY_TARGET) / HY_SLOW, HY_W, dtype=F32))[None]
    row = lambda a: a.astype(F32)[None]


    def place(a, r0, c0, cols):
        return jnp.pad(a.astype(F32), ((r0, LANE - r0 - a.shape[0]), (c0, cols - c0 - a.shape[1])))

    layers = []
    for l in range(norm_mix.shape[0]):
        w = w_in[l]
        layers.append(dict(
            norm_mix=row(norm_mix[l]),
            wq=w[:, :o1].astype(BF16),
            wk=_dup_heads(w[:, o1:o2]).astype(BF16),
            wv=_dup_heads(w[:, o2:o3]).astype(BF16),
            wh=w[:, o3:o4].astype(BF16),
            wp=w[:, o4:o5].astype(BF16),
            wgates=w[:, o5:].astype(BF16),
            q_gain=jnp.tile(row(q_gain[l]), (1, N_HEADS)) * (HEAD_DIM ** -0.5 * LOG2E),
            k_gain=jnp.tile(row(k_gain[l]), (1, 2 * N_KV)),
            e=e,
            sink=attn_sink[l].astype(F32) * LOG2E,
            hy_conv_w=hy_conv_w[l].astype(F32), hy_conv_b=row(hy_conv_b[l]),
            hy_w1=jnp.stack([place(hy_w1[l], 0, 0, LANE), place(hy_w1[l], 0, HY_HID, LANE)]),
            hy_b1=jnp.tile(row(hy_b1[l]), (1, 2)),
            hy_w2=place(hy_w2[l], 0, 0, LANE) + place(hy_w2[l], HY_HID, HY_HID, LANE),
            hy_b2=jnp.tile(row(hy_b2[l]), (1, 2)),
            hy_w3=jnp.stack([place(hy_w3[l], 0, 0, 2 * HY_W), place(hy_w3[l], HY_HID, 0, 2 * HY_W)]),
            hy_freq=jnp.tile(row(hy_freq[l]), (1, 2)),
            hy_skip=row(hy_skip[l]), deltas=deltas,
            pool_w=pool_w[l].astype(BF16), pool_scale=row(pool_scale[l]),
            w_attn_o=w_attn_o[l].astype(BF16), w_hyena_o=w_hyena_o[l].astype(BF16),
            w_pool_o=w_pool_o[l].astype(BF16), w_out=w_out[l].astype(BF16),
            norm_ffn=row(norm_ffn[l]), w_gate=w_gate[l].astype(BF16), w_up=w_up[l].astype(BF16),
            ffn_conv_w=ffn_conv_w[l].astype(F32), ffn_conv_b=row(ffn_conv_b[l]),
            w_down=w_down[l].astype(BF16),
        ))
    return layers


def kernel(x_prompt, x_sample, meta_tokens, rel_bias, norm_mix, w_in, q_gain, k_gain, attn_sink,
           hy_conv_w, hy_conv_b, hy_w1, hy_b1, hy_w2, hy_b2, hy_w3, hy_freq, hy_skip, pool_w,
           pool_scale, w_attn_o, w_hyena_o, w_pool_o, w_out, norm_ffn, w_gate, w_up,
           ffn_conv_w, ffn_conv_b, w_down):
    layers = _prep_layers(norm_mix, w_in, q_gain, k_gain, attn_sink, hy_conv_w, hy_conv_b, hy_w1,
                          hy_b1, hy_w2, hy_b2, hy_w3, hy_freq, hy_skip, pool_w, pool_scale,
                          w_attn_o, w_hyena_o, w_pool_o, w_out, norm_ffn, w_gate, w_up,
                          ffn_conv_w, ffn_conv_b, w_down)
    bias = _bias_table(rel_bias)
    meta = meta_tokens.astype(F32)
    return (_trunk(x_prompt, meta, bias, layers), _trunk(x_sample, meta, bias, layers))
```
